```python
import jax, jax.numpy as jnp
from jax import lax
import numpy as np

D_MODEL = 4096
BATCH = 1
SEQ = 8192
DEPTH = 1

CONV_WIDTH = D_MODEL // 2
CONV_K = 3
POOL_WINDOWS = (2, 4, 8, 16)
N_POOL_GROUPS = len(POOL_WINDOWS)
POOL_WIDTH = D_MODEL // 2
POOL_GROUP = POOL_WIDTH // N_POOL_GROUPS
N_BRANCHES = 2
FFN_HIDDEN = ((8 * D_MODEL // 3 + 255) // 256) * 256
N_MOD = 6
IN_COLS = 3 * CONV_WIDTH + POOL_WIDTH + N_BRANCHES * D_MODEL
EPS = 1e-6

kernel_name = "hybrid_conv_pool_gated_block"


def rms_norm(x, gain):
    x32 = x.astype(jnp.float32)
    y = x32 * lax.rsqrt(jnp.mean(x32 * x32, axis=-1, keepdims=True) + EPS)
    return (y * gain.astype(jnp.float32)).astype(x.dtype)


def modulate(h, shift, scale):
    return h * (1.0 + scale[:, None, :]) + shift[:, None, :]


def causal_short_conv(u, w):
    s = u.shape[1]
    k_width = w.shape[0]
    up = jnp.pad(u, ((0, 0), (k_width - 1, 0), (0, 0)))
    y = w[0] * up[:, 0:s]
    for k in range(1, k_width):
        y = y + w[k] * up[:, k:k + s]
    return y


def causal_mean_minus_self(u, window):
    b, s, ch = u.shape
    u32 = u.astype(jnp.float32)
    cs = jnp.concatenate([jnp.zeros((b, 1, ch), jnp.float32), jnp.cumsum(u32, axis=1)], axis=1)
    hi = cs[:, 1:]
    lo = jnp.pad(cs[:, :s + 1 - window], ((0, 0), (window - 1, 0), (0, 0)))
    count = jnp.minimum(jnp.arange(1, s + 1), window).astype(jnp.float32)[None, :, None]
    return ((hi - lo) / count - u32).astype(u.dtype)


def hybrid_mixer(h, w_in, b_branch_gate, conv_w, w_conv_out, w_pool_group, pool_scale, w_pool_out, w_o):
    b, s, _ = h.shape
    proj = jnp.einsum('bsd,dn->bsn', h, w_in)
    c0 = CONV_WIDTH
    splits = [c0, 2 * c0, 3 * c0, 3 * c0 + POOL_WIDTH, 3 * c0 + POOL_WIDTH + D_MODEL]
    gate_b, gate_c, v, p, g_conv, g_pool = jnp.split(proj, splits, axis=-1)

    y_conv = jnp.einsum('bsc,cd->bsd', gate_b * causal_short_conv(gate_c * v, conv_w), w_conv_out)

    pg = p.reshape(b, s, N_POOL_GROUPS, POOL_GROUP)
    pooled = jnp.stack([causal_mean_minus_self(pg[:, :, g], POOL_WINDOWS[g]) for g in range(N_POOL_GROUPS)], axis=2)
    mixed = jnp.einsum('bsgc,gce->bsge', pooled, w_pool_group).reshape(b, s, POOL_WIDTH) * pool_scale
    y_pool = jnp.einsum('bsc,cd->bsd', mixed, w_pool_out)

    gb_conv, gb_pool = jnp.split(b_branch_gate, 2)
    merged = jax.nn.sigmoid(g_conv + gb_conv) * y_conv + jax.nn.sigmoid(g_pool + gb_pool) * y_pool
    return jnp.einsum('bsd,de->bse', merged, w_o)


def swiglu_ffn(h, w_gate_up, w_down):
    gu = jnp.einsum('bsd,df->bsf', h, w_gate_up)
    g, u = jnp.split(gu, 2, axis=-1)
    return jnp.einsum('bsf,fd->bsd', jax.nn.silu(g) * u, w_down)


def setup_inputs(seed: int = 0) -> dict:
    key = jax.random.key(seed)
    ks = jax.random.split(key, 20)
    f32 = jnp.float32
    nrm = lambda k, shape, scale: jax.random.normal(k, shape, f32) * scale
    return {
        "x": nrm(ks[0], (BATCH, SEQ, D_MODEL), 1.0),
        "c": nrm(ks[1], (BATCH, D_MODEL), 1.0),
        "w_ada": nrm(ks[2], (DEPTH, D_MODEL, N_MOD * D_MODEL), D_MODEL ** -0.5),
        "b_ada": nrm(ks[3], (DEPTH, N_MOD * D_MODEL), 0.02),
        "norm1_gain": 1.0 + nrm(ks[4], (DEPTH, D_MODEL), 0.05),
        "w_in": nrm(ks[5], (DEPTH, D_MODEL, IN_COLS), D_MODEL ** -0.5),
        "b_branch_gate": nrm(ks[6], (DEPTH, N_BRANCHES * D_MODEL), 0.02),
        "conv_w": nrm(ks[7], (DEPTH, CONV_K, CONV_WIDTH), CONV_K ** -0.5),
        "w_conv_out": nrm(ks[8], (DEPTH, CONV_WIDTH, D_MODEL), CONV_WIDTH ** -0.5),
        "w_pool_group": nrm(ks[9], (DEPTH, N_POOL_GROUPS, POOL_GROUP, POOL_GROUP), POOL_GROUP ** -0.5),
        "pool_scale": 1.0 + nrm(ks[10], (DEPTH, POOL_WIDTH), 0.05),
        "w_pool_out": nrm(ks[11], (DEPTH, POOL_WIDTH, D_MODEL), POOL_WIDTH ** -0.5),
        "w_o": nrm(ks[12], (DEPTH, D_MODEL, D_MODEL), D_MODEL ** -0.5),
        "norm2_gain": 1.0 + nrm(ks[13], (DEPTH, D_MODEL), 0.05),
        "w_gate_up": nrm(ks[14], (DEPTH, D_MODEL, 2 * FFN_HIDDEN), D_MODEL ** -0.5),
        "w_down": nrm(ks[15], (DEPTH, FFN_HIDDEN, D_MODEL), FFN_HIDDEN ** -0.5),
        "final_norm_gain": 1.0 + nrm(ks[16], (D_MODEL,), 0.05),
    }


def reference(x, c, w_ada, b_ada, norm1_gain, w_in, b_branch_gate, conv_w, w_conv_out, w_pool_group,
              pool_scale, w_pool_out, w_o, norm2_gain, w_gate_up, w_down, final_norm_gain):
    cs = jax.nn.silu(c)
    for l in range(DEPTH):
        mod = jnp.einsum('bd,dm->bm', cs, w_ada[l]) + b_ada[l]
        shift1, scale1, gate1, shift2, scale2, gate2 = jnp.split(mod, N_MOD, axis=-1)

        h = modulate(rms_norm(x, norm1_gain[l]), shift1, scale1)
        y = hybrid_mixer(h, w_in[l], b_branch_gate[l], conv_w[l], w_conv_out[l], w_pool_group[l],
                         pool_scale[l], w_pool_out[l], w_o[l])
        x = x + gate1[:, None, :] * y

        h = modulate(rms_norm(x, norm2_gain[l]), shift2, scale2)
        x = x + gate2[:, None, :] * swiglu_ffn(h, w_gate_up[l], w_down[l])
    return rms_norm(x, final_norm_gain)
```

```python
import functools

import jax
import jax.numpy as jnp
from jax.experimental import pallas as pl
from jax.experimental.pallas import tpu as pltpu

F32 = jnp.float32
BF16 = jnp.bfloat16

EPS = 1e-6
CONV_K = 3
POOL_WINDOWS = (2, 4, 8, 16)
HALO = 16
V7X_VMEM_LIMIT = 56 * 1024 * 1024


def _params(*sem):
    return pltpu.CompilerParams(dimension_semantics=sem, vmem_limit_bytes=V7X_VMEM_LIMIT)


def _ada_kernel(c_ref, w_ref, b_ref, o_ref):
    c = c_ref[...]
    cs = c * jax.nn.sigmoid(c)
    o_ref[...] = jnp.sum(w_ref[...] * cs, axis=0, keepdims=True) + b_ref[...]


def _ada(c_col, w_ada, b_ada, tn=512):
    d, n = w_ada.shape
    return pl.pallas_call(
        _ada_kernel,
        grid=(n // tn,),
        in_specs=[pl.BlockSpec((d, 1), lambda j: (0, 0)),
                  pl.BlockSpec((d, tn), lambda j: (0, j)),
                  pl.BlockSpec((1, tn), lambda j: (0, j))],
        out_specs=pl.BlockSpec((1, tn), lambda j: (0, j)),
        out_shape=jax.ShapeDtypeStruct((1, n), F32),
        compiler_params=_params("arbitrary"),
        name="ada_matvec",
    )(c_col, w_ada, b_ada)


def _norm_mod_kernel(x_ref, g_ref, shift_ref, scale_ref, o_ref):
    x = x_ref[...]
    r = jax.lax.rsqrt(jnp.mean(x * x, axis=-1, keepdims=True) + EPS)
    y = (x * r) * g_ref[...]
    o_ref[...] = (y * (1.0 + scale_ref[...]) + shift_ref[...]).astype(o_ref.dtype)


def _norm_mod(x, gain, mod, shift_blk, scale_blk, tm=512):
    s, d = x.shape
    return pl.pallas_call(
        _norm_mod_kernel,
        grid=(s // tm,),
        in_specs=[pl.BlockSpec((tm, d), lambda i: (i, 0)),
                  pl.BlockSpec((1, d), lambda i: (0, 0)),
                  pl.BlockSpec((1, d), lambda i: (0, shift_blk)),
                  pl.BlockSpec((1, d), lambda i: (0, scale_blk))],
        out_specs=pl.BlockSpec((tm, d), lambda i: (i, 0)),
        out_shape=jax.ShapeDtypeStruct((s, d), BF16),
        compiler_params=_params("arbitrary"),
        name="norm_modulate",
    )(x, gain, mod, mod)


def _norm_kernel(x_ref, g_ref, o_ref):
    x = x_ref[...]
    r = jax.lax.rsqrt(jnp.mean(x * x, axis=-1, keepdims=True) + EPS)
    o_ref[...] = (x * r) * g_ref[...]


def _norm(x, gain, tm=512):
    s, d = x.shape
    return pl.pallas_call(
        _norm_kernel,
        grid=(s // tm,),
        in_specs=[pl.BlockSpec((tm, d), lambda i: (i, 0)),
                  pl.BlockSpec((1, d), lambda i: (0, 0))],
        out_specs=pl.BlockSpec((tm, d), lambda i: (i, 0)),
        out_shape=jax.ShapeDtypeStruct((s, d), F32),
        compiler_params=_params("arbitrary"),
        name="final_norm",
    )(x, gain)


def _cast_kernel(x_ref, o_ref):
    o_ref[...] = x_ref[...].astype(o_ref.dtype)


def _cast_bf16(w, tr=256):
    r, c = w.shape
    return pl.pallas_call(
        _cast_kernel,
        grid=(r // tr,),
        in_specs=[pl.BlockSpec((tr, c), lambda i: (i, 0))],
        out_specs=pl.BlockSpec((tr, c), lambda i: (i, 0)),
        out_shape=jax.ShapeDtypeStruct((r, c), BF16),
        compiler_params=_params("arbitrary"),
        name="cast_bf16",
    )(w)


def _proj_kernel(h_ref, w_ref, o_ref, wbf_ref):
    @pl.when(pl.program_id(1) == 0)
    def _():
        wbf_ref[...] = w_ref[...].astype(BF16)

    acc = jnp.dot(h_ref[...], wbf_ref[...], preferred_element_type=F32)
    o_ref[...] = acc.astype(o_ref.dtype)


def _proj_gate_kernel(h_ref, w_ref, b_ref, o_ref, wbf_ref):
    @pl.when(pl.program_id(1) == 0)
    def _():
        wbf_ref[...] = w_ref[...].astype(BF16)

    acc = jnp.dot(h_ref[...], wbf_ref[...], preferred_element_type=F32)
    o_ref[...] = jax.nn.sigmoid(acc + b_ref[...]).astype(o_ref.dtype)


def _in_proj(h, w_in, col0, ncols, bias=None, tm=1024, tn=512):
    s, d = h.shape
    off = col0 // tn
    in_specs = [pl.BlockSpec((tm, d), lambda j, i: (i, 0)),
                pl.BlockSpec((d, tn), lambda j, i: (0, j + off))]
    args = [h, w_in]
    body = _proj_kernel
    if bias is not None:
        in_specs.append(pl.BlockSpec((1, tn), lambda j, i: (0, j)))
        args.append(bias)
        body = _proj_gate_kernel
    return pl.pallas_call(
        body,
        grid=(ncols // tn, s // tm),
        in_specs=in_specs,
        out_specs=pl.BlockSpec((tm, tn), lambda j, i: (i, j)),
        out_shape=jax.ShapeDtypeStruct((s, ncols), BF16),
        scratch_shapes=[pltpu.VMEM((d, tn), BF16)],
        compiler_params=_params("arbitrary", "arbitrary"),
        name="in_proj_gate" if bias is not None else "in_proj",
    )(*args)


def _shift_rows(ext, k):
    return pltpu.roll(ext, k, axis=0)


def _mixer_prep_kernel(gb_ref, gc_ref, v_ref, p_ref, gcp_ref, vp_ref, pp_ref,
                       cw_ref, wg_ref, ps_ref, a_ref, mixed_ref, wgbf_ref, *, tm, cw, ng):
    i = pl.program_id(0)

    @pl.when(i == 0)
    def _():
        wgbf_ref[...] = wg_ref[...].astype(BF16)

    has_hist = (i > 0).astype(F32)
    pg = p_ref.shape[1] // ng
    row = jax.lax.broadcasted_iota(jnp.int32, (tm, 1), 0) + i * tm

    for c in range(0, gb_ref.shape[1], cw):
        sl = slice(c, c + cw)
        u = gc_ref[:, sl].astype(F32) * v_ref[:, sl].astype(F32)
        up = gcp_ref[:, sl].astype(F32) * vp_ref[:, sl].astype(F32) * has_hist
        ext = jnp.concatenate([up, u], axis=0)
        y = cw_ref[0:1, sl] * _shift_rows(ext, 2)[HALO:]
        y = y + cw_ref[1:2, sl] * _shift_rows(ext, 1)[HALO:]
        y = y + cw_ref[2:3, sl] * u
        a_ref[:, sl] = (gb_ref[:, sl].astype(F32) * y).astype(a_ref.dtype)

    for g in range(ng):
        sl = slice(g * pg, (g + 1) * pg)
        window = POOL_WINDOWS[g]
        cur = p_ref[:, sl].astype(F32)
        ext = jnp.concatenate([pp_ref[:, sl].astype(F32) * has_hist, cur], axis=0)
        k = 1
        while k < window:
            ext = ext + _shift_rows(ext, k)
            k *= 2
        count = jnp.minimum(row + 1, window).astype(F32)
        pooled = ext[HALO:] / count - cur
        mixed = jnp.dot(pooled.astype(BF16), wgbf_ref[g], preferred_element_type=F32)
        mixed_ref[:, sl] = (mixed * ps_ref[:, sl]).astype(mixed_ref.dtype)


def _mixer_prep(proj, conv_w, w_pool_group, pool_scale, cwid, tm=256, cw=512):
    s = proj.shape[0]
    ng, pg, _ = w_pool_group.shape
    hb = tm // HALO
    cur = lambda blk: pl.BlockSpec((tm, cwid), lambda i: (i, blk))
    prev = lambda blk: pl.BlockSpec((HALO, cwid), lambda i: (jnp.maximum(i * hb - 1, 0), blk))
    return pl.pallas_call(
        functools.partial(_mixer_prep_kernel, tm=tm, cw=cw, ng=ng),
        grid=(s // tm,),
        in_specs=[cur(0), cur(1), cur(2), cur(3), prev(1), prev(2), prev(3),
                  pl.BlockSpec((CONV_K, cwid), lambda i: (0, 0)),
                  pl.BlockSpec((ng, pg, pg), lambda i: (0, 0, 0)),
                  pl.BlockSpec((1, cwid), lambda i: (0, 0))],
        out_specs=[pl.BlockSpec((tm, cwid), lambda i: (i, 0)),
                   pl.BlockSpec((tm, cwid), lambda i: (i, 0))],
        out_shape=[jax.ShapeDtypeStruct((s, cwid), BF16), jax.ShapeDtypeStruct((s, cwid), BF16)],
        scratch_shapes=[pltpu.VMEM((ng, pg, pg), BF16)],
        compiler_params=_params("arbitrary"),
        name="mixer_prep",
    )(proj, proj, proj, proj, proj, proj, proj, conv_w, w_pool_group, pool_scale)


def _merge_kernel(a_ref, m_ref, sgc_ref, sgp_ref, wc_ref, wp_ref, o_ref, wcbf_ref, wpbf_ref):
    @pl.when(pl.program_id(1) == 0)
    def _():
        wcbf_ref[...] = wc_ref[...].astype(BF16)
        wpbf_ref[...] = wp_ref[...].astype(BF16)

    y_conv = jnp.dot(a_ref[...], wcbf_ref[...], preferred_element_type=F32)
    y_pool = jnp.dot(m_ref[...], wpbf_ref[...], preferred_element_type=F32)
    merged = sgc_ref[...].astype(F32) * y_conv + sgp_ref[...].astype(F32) * y_pool
    o_ref[...] = merged.astype(o_ref.dtype)


def _merge(a, mixed, sg, w_conv_out, w_pool_out, tm=1024, tn=512):
    s, k = a.shape
    d = w_conv_out.shape[1]
    nj = d // tn
    return pl.pallas_call(
        _merge_kernel,
        grid=(nj, s // tm),
        in_specs=[pl.BlockSpec((tm, k), lambda j, i: (i, 0)),
                  pl.BlockSpec((tm, k), lambda j, i: (i, 0)),
                  pl.BlockSpec((tm, tn), lambda j, i: (i, j)),
                  pl.BlockSpec((tm, tn), lambda j, i: (i, j + nj)),
                  pl.BlockSpec((k, tn), lambda j, i: (0, j)),
                  pl.BlockSpec((k, tn), lambda j, i: (0, j))],
        out_specs=pl.BlockSpec((tm, tn), lambda j, i: (i, j)),
        out_shape=jax.ShapeDtypeStruct((s, d), BF16),
        scratch_shapes=[pltpu.VMEM((k, tn), BF16), pltpu.VMEM((k, tn), BF16)],
        compiler_params=_params("arbitrary", "arbitrary"),
        name="branch_merge",
    )(a, mixed, sg, sg, w_conv_out, w_pool_out)


def _resid_kernel(l_ref, w_ref, x_ref, g_ref, o_ref, wbf_ref):
    @pl.when(pl.program_id(1) == 0)
    def _():
        wbf_ref[...] = w_ref[...].astype(BF16)

    y = jnp.dot(l_ref[...], wbf_ref[...], preferred_element_type=F32)
    o_ref[...] = x_ref[...] + g_ref[...] * y


def _matmul_resid(lhs, w, x, mod, gate_blk, tm, tn):
    s, k = lhs.shape
    d = w.shape[1]
    goff = gate_blk * (d // tn)
    return pl.pallas_call(
        _resid_kernel,
        grid=(d // tn, s // tm),
        in_specs=[pl.BlockSpec((tm, k), lambda j, i: (i, 0)),
                  pl.BlockSpec((k, tn), lambda j, i: (0, j)),
                  pl.BlockSpec((tm, tn), lambda j, i: (i, j)),
                  pl.BlockSpec((1, tn), lambda j, i: (0, j + goff))],
        out_specs=pl.BlockSpec((tm, tn), lambda j, i: (i, j)),
        out_shape=jax.ShapeDtypeStruct((s, d), F32),
        scratch_shapes=[pltpu.VMEM((k, tn), BF16)],
        compiler_params=_params("arbitrary", "arbitrary"),
        name="matmul_residual",
    )(lhs, w, x, mod)


def _swiglu_kernel(h_ref, wg_ref, wu_ref, o_ref, wbf_ref, *, tn):
    @pl.when(pl.program_id(1) == 0)
    def _():
        wbf_ref[:, :tn] = wg_ref[...].astype(BF16)
        wbf_ref[:, tn:] = wu_ref[...].astype(BF16)

    gu = jnp.dot(h_ref[...], wbf_ref[...], preferred_element_type=F32)
    g = gu[:, :tn]
    o_ref[...] = (g * jax.nn.sigmoid(g) * gu[:, tn:]).astype(o_ref.dtype)


def _swiglu_up(h, w_gate_up, tm=1024, tn=256):
    s, d = h.shape
    f = w_gate_up.shape[1] // 2
    nj = f // tn
    return pl.pallas_call(
        functools.partial(_swiglu_kernel, tn=tn),
        grid=(nj, s // tm),
        in_specs=[pl.BlockSpec((tm, d), lambda j, i: (i, 0)),
                  pl.BlockSpec((d, tn), lambda j, i: (0, j)),
                  pl.BlockSpec((d, tn), lambda j, i: (0, j + nj))],
        out_specs=pl.BlockSpec((tm, tn), lambda j, i: (i, j)),
        out_shape=jax.ShapeDtypeStruct((s, f), BF16),
        scratch_shapes=[pltpu.VMEM((d, 2 * tn), BF16)],
        compiler_params=_params("arbitrary", "arbitrary"),
        name="swiglu_up",
    )(h, w_gate_up, w_gate_up)


def _down_kernel(l_ref, w_ref, x_ref, g_ref, o_ref):
    y = jnp.dot(l_ref[...], w_ref[...], preferred_element_type=F32)
    o_ref[...] = x_ref[...] + g_ref[...] * y


def _down_resid(act, w_bf, x, mod, gate_blk, tm=512, tn=512):
    s, k = act.shape
    d = w_bf.shape[1]
    goff = gate_blk * (d // tn)
    return pl.pallas_call(
        _down_kernel,
        grid=(d // tn, s // tm),
        in_specs=[pl.BlockSpec((tm, k), lambda j, i: (i, 0)),
                  pl.BlockSpec((k, tn), lambda j, i: (0, j)),
                  pl.BlockSpec((tm, tn), lambda j, i: (i, j)),
                  pl.BlockSpec((1, tn), lambda j, i: (0, j + goff))],
        out_specs=pl.BlockSpec((tm, tn), lambda j, i: (i, j)),
        out_shape=jax.ShapeDtypeStruct((s, d), F32),
        compiler_params=_params("arbitrary", "arbitrary"),
        name="down_residual",
    )(act, w_bf, x, mod)


def kernel(x, c, w_ada, b_ada, norm1_gain, w_in, b_branch_gate, conv_w, w_conv_out, w_pool_group,
           pool_scale, w_pool_out, w_o, norm2_gain, w_gate_up, w_down, final_norm_gain):
    batch, seq, d = x.shape
    depth = w_ada.shape[0]
    cwid = conv_w.shape[2]
    outs = []
    for b in range(batch):
        xb = x[b]
        c_col = c[b].reshape(d, 1)
        for l in range(depth):
            mod = _ada(c_col, w_ada[l], b_ada[l].reshape(1, -1))

            h = _norm_mod(xb, norm1_gain[l].reshape(1, d), mod, 0, 1)
            proj = _in_proj(h, w_in[l], 0, 4 * cwid)
            sg = _in_proj(h, w_in[l], 4 * cwid, 2 * d, bias=b_branch_gate[l].reshape(1, -1))
            a, mixed = _mixer_prep(proj, conv_w[l], w_pool_group[l], pool_scale[l].reshape(1, -1), cwid)
            merged = _merge(a, mixed, sg, w_conv_out[l], w_pool_out[l])
            xb = _matmul_resid(merged, w_o[l], xb, mod, 2, tm=1024, tn=512)

            h = _norm_mod(xb, norm2_gain[l].reshape(1, d), mod, 3, 4)
            act = _swiglu_up(h, w_gate_up[l])
            xb = _down_resid(act, _cast_bf16(w_down[l]), xb, mod, 5)
        outs.append(_norm(xb, final_norm_gain.reshape(1, d)).reshape(1, seq, d))
    return outs[0] if batch == 1 else jnp.concatenate(outs, axis=0)
```

```python
import functools

import jax
import jax.numpy as jnp
from jax.experimental import pallas as pl
from jax.experimental.pallas import tpu as pltpu

F32 = jnp.float32
BF16 = jnp.bfloat16

EPS = 1e-6
CONV_K = 3
POOL_WINDOWS = (2, 4, 8, 16)
HALO = 16
LANE = 128
BF16_SUBLANE = 16
V7X_VMEM_LIMIT = 56 * 1024 * 1024


def _params(*sem):
    return pltpu.CompilerParams(dimension_semantics=sem, vmem_limit_bytes=V7X_VMEM_LIMIT)


def _ada_kernel(c_ref, w_ref, b_ref, o_ref):
    c = c_ref[...]
    cs = c * jax.nn.sigmoid(c)
    o_ref[...] = jnp.sum(w_ref[...] * cs, axis=0, keepdims=True) + b_ref[...]


def _ada(c_col, w_ada, b_ada, tn=512):
    d, n = w_ada.shape
    return pl.pallas_call(
        _ada_kernel,
        grid=(n // tn,),
        in_specs=[pl.BlockSpec((d, 1), lambda j: (0, 0)),
                  pl.BlockSpec((d, tn), lambda j: (0, j)),
                  pl.BlockSpec((1, tn), lambda j: (0, j))],
        out_specs=pl.BlockSpec((1, tn), lambda j: (0, j)),
        out_shape=jax.ShapeDtypeStruct((1, n), F32),
        compiler_params=_params("arbitrary"),
        name="ada_matvec",
    )(c_col, w_ada, b_ada)


def _norm_mod_kernel(x_ref, g_ref, shift_ref, scale_ref, o_ref):
    x = x_ref[...]
    r = jax.lax.rsqrt(jnp.mean(x * x, axis=-1, keepdims=True) + EPS)
    y = (x * r) * g_ref[...]
    o_ref[...] = (y * (1.0 + scale_ref[...]) + shift_ref[...]).astype(o_ref.dtype)


def _norm_mod(x, gain, mod, shift_blk, scale_blk, tm=512):
    s, d = x.shape
    return pl.pallas_call(
        _norm_mod_kernel,
        grid=(s // tm,),
        in_specs=[pl.BlockSpec((tm, d), lambda i: (i, 0)),
                  pl.BlockSpec((1, d), lambda i: (0, 0)),
                  pl.BlockSpec((1, d), lambda i: (0, shift_blk)),
                  pl.BlockSpec((1, d), lambda i: (0, scale_blk))],
        out_specs=pl.BlockSpec((tm, d), lambda i: (i, 0)),
        out_shape=jax.ShapeDtypeStruct((s, d), BF16),
        compiler_params=_params("arbitrary"),
        name="norm_modulate",
    )(x, gain, mod, mod)


def _norm_kernel(x_ref, g_ref, o_ref):
    x = x_ref[...]
    r = jax.lax.rsqrt(jnp.mean(x * x, axis=-1, keepdims=True) + EPS)
    o_ref[...] = (x * r) * g_ref[...]


def _norm(x, gain, tm=512):
    s, d = x.shape
    return pl.pallas_call(
        _norm_kernel,
        grid=(s // tm,),
        in_specs=[pl.BlockSpec((tm, d), lambda i: (i, 0)),
                  pl.BlockSpec((1, d), lambda i: (0, 0))],
        out_specs=pl.BlockSpec((tm, d), lambda i: (i, 0)),
        out_shape=jax.ShapeDtypeStruct((s, d), F32),
        compiler_params=_params("arbitrary"),
        name="final_norm",
    )(x, gain)


def _chunk_copies(w_hbm, stage, sems, parts, tc, ck, tile, chunk, slot):
    copies = []
    for p, (row0, col0) in enumerate(parts):
        col = col0 + tile * tc
        if not isinstance(col, int):
            col = pl.multiple_of(col, LANE)
        row = row0 + chunk * ck
        if not isinstance(row, int):
            row = pl.multiple_of(row, BF16_SUBLANE)
        copies.append(pltpu.make_async_copy(
            w_hbm.at[pl.ds(row, ck), pl.ds(col, tc)],
            stage.at[slot, :, pl.ds(p * tc, tc)],
            sems.at[slot, p]))
    return copies


def _stream_kernel(*refs, n_ops, n_extra, n_out, parts, tc, ck, n_i, n_j, epilogue):
    lhs = refs[:n_ops]
    extra = refs[n_ops:n_ops + n_extra]
    w_hbm = refs[n_ops + n_extra:2 * n_ops + n_extra]
    outs = refs[2 * n_ops + n_extra:2 * n_ops + n_extra + n_out]
    scratch = refs[2 * n_ops + n_extra + n_out:]
    wbf, stage, sems = scratch[0::3], scratch[1::3], scratch[2::3]
    j = pl.program_id(0)
    i = pl.program_id(1)
    cur = j % 2

    def copies(o, tile, chunk, slot):
        return _chunk_copies(w_hbm[o], stage[o], sems[o], parts[o], tc, ck[o], tile, chunk, slot)

    @pl.when((j == 0) & (i == 0))
    def _load_first_tile():
        for o in range(n_ops):
            for cp in copies(o, 0, 0, 0):
                cp.start()
            for c in range(n_i):
                if c + 1 < n_i:
                    for cp in copies(o, 0, c + 1, (c + 1) % 2):
                        cp.start()
                for cp in copies(o, 0, c, c % 2):
                    cp.wait()
                wbf[o][0, c * ck[o]:(c + 1) * ck[o], :] = stage[o][c % 2].astype(BF16)
            if n_j > 1:
                for cp in copies(o, 1, 0, 0):
                    cp.start()

    @pl.when(j + 1 < n_j)
    def _stream_next_tile():
        slot = i % 2
        for o in range(n_ops):
            for cp in copies(o, j + 1, i, slot):
                cp.wait()

            @pl.when(i + 1 < n_i)
            def _():
                for cp in copies(o, j + 1, i + 1, 1 - slot):
                    cp.start()

            @pl.when((i + 1 == n_i) & (j + 2 < n_j))
            def _():
                for cp in copies(o, j + 2, 0, 0):
                    cp.start()

            row = pl.multiple_of(i * ck[o], BF16_SUBLANE)
            wbf[o][1 - cur, pl.ds(row, ck[o]), :] = stage[o][slot].astype(BF16)

    accs = [jnp.dot(lhs[o][...], wbf[o][cur], preferred_element_type=F32) for o in range(n_ops)]
    epilogue(accs, extra, outs)


def _stream_matmul(name, lhs, weights, parts, tc, n_j, tm, epilogue, extra=(), extra_specs=(),
                   out_shapes=(), out_specs=(), aliases=None):
    n_ops = len(lhs)
    s = lhs[0].shape[0]
    n_i = s // tm
    assert s % tm == 0 and n_i % 2 == 0, "stage slots alternate per row tile"
    ck = []
    in_specs = [pl.BlockSpec((tm, l.shape[1]), lambda j, i: (i, 0)) for l in lhs]
    in_specs += list(extra_specs)
    in_specs += [pl.BlockSpec(memory_space=pl.ANY) for _ in weights]
    scratch = []
    for o in range(n_ops):
        k = lhs[o].shape[1]
        assert lhs[o].shape[0] == s and k % (n_i * BF16_SUBLANE) == 0
        ck.append(k // n_i)
        tn = len(parts[o]) * tc
        scratch += [pltpu.VMEM((2, k, tn), BF16),
                    pltpu.VMEM((2, ck[o], tn), F32),
                    pltpu.SemaphoreType.DMA((2, len(parts[o])))]
    body = functools.partial(_stream_kernel, n_ops=n_ops, n_extra=len(extra), n_out=len(out_shapes),
                             parts=parts, tc=tc, ck=tuple(ck), n_i=n_i, n_j=n_j, epilogue=epilogue)
    return pl.pallas_call(
        body,
        grid=(n_j, n_i),
        in_specs=in_specs,
        out_specs=list(out_specs),
        out_shape=list(out_shapes),
        scratch_shapes=scratch,
        input_output_aliases=aliases or {},
        compiler_params=_params("arbitrary", "arbitrary"),
        name=name,
    )(*lhs, *extra, *weights)


def _store_bf16(accs, extra, outs):
    outs[0][...] = accs[0].astype(outs[0].dtype)


def _store_sigmoid(accs, extra, outs):
    outs[0][...] = jax.nn.sigmoid(accs[0] + extra[0][...]).astype(outs[0].dtype)


def _in_proj(h, w_in, col0, ncols, bias=None, tm=1024, tn=1024):
    s = h.shape[0]
    gated = bias is not None
    return _stream_matmul(
        "in_proj_gate" if gated else "in_proj", [h], [w_in], parts=(((0, col0),),), tc=tn, n_j=ncols // tn, tm=tm,
        epilogue=_store_sigmoid if gated else _store_bf16,
        extra=(bias,) if gated else (),
        extra_specs=(pl.BlockSpec((1, tn), lambda j, i: (0, j)),) if gated else (),
        out_shapes=(jax.ShapeDtypeStruct((s, ncols), BF16),),
        out_specs=(pl.BlockSpec((tm, tn), lambda j, i: (i, j)),))[0]


def _shift_rows(ext, k):
    return pltpu.roll(ext, k, axis=0)


def _mixer_prep_kernel(gb_ref, gc_ref, v_ref, p_ref, gcp_ref, vp_ref, pp_ref,
                       cw_ref, wg_ref, ps_ref, a_ref, mixed_ref, wgbf_ref, *, tm, cw, ng):
    i = pl.program_id(0)

    @pl.when(i == 0)
    def _():
        wgbf_ref[...] = wg_ref[...].astype(BF16)

    has_hist = (i > 0).astype(F32)
    pg = p_ref.shape[1] // ng
    row = jax.lax.broadcasted_iota(jnp.int32, (tm, 1), 0) + i * tm

    for c in range(0, gb_ref.shape[1], cw):
        sl = slice(c, c + cw)
        u = gc_ref[:, sl].astype(F32) * v_ref[:, sl].astype(F32)
        up = gcp_ref[:, sl].astype(F32) * vp_ref[:, sl].astype(F32) * has_hist
        ext = jnp.concatenate([up, u], axis=0)
        y = cw_ref[0:1, sl] * _shift_rows(ext, 2)[HALO:]
        y = y + cw_ref[1:2, sl] * _shift_rows(ext, 1)[HALO:]
        y = y + cw_ref[2:3, sl] * u
        a_ref[:, sl] = (gb_ref[:, sl].astype(F32) * y).astype(a_ref.dtype)

    for g in range(ng):
        sl = slice(g * pg, (g + 1) * pg)
        window = POOL_WINDOWS[g]
        cur = p_ref[:, sl].astype(F32)
        ext = jnp.concatenate([pp_ref[:, sl].astype(F32) * has_hist, cur], axis=0)
        k = 1
        while k < window:
            ext = ext + _shift_rows(ext, k)
            k *= 2
        count = jnp.minimum(row + 1, window).astype(F32)
        pooled = ext[HALO:] / count - cur
        mixed = jnp.dot(pooled.astype(BF16), wgbf_ref[g], preferred_element_type=F32)
        mixed_ref[:, sl] = (mixed * ps_ref[:, sl]).astype(mixed_ref.dtype)


def _mixer_prep(proj, conv_w, w_pool_group, pool_scale, cwid, tm=256, cw=512):
    s = proj.shape[0]
    ng, pg, _ = w_pool_group.shape
    hb = tm // HALO
    cur = lambda blk: pl.BlockSpec((tm, cwid), lambda i: (i, blk))
    prev = lambda blk: pl.BlockSpec((HALO, cwid), lambda i: (jnp.maximum(i * hb - 1, 0), blk))
    return pl.pallas_call(
        functools.partial(_mixer_prep_kernel, tm=tm, cw=cw, ng=ng),
        grid=(s // tm,),
        in_specs=[cur(0), cur(1), cur(2), cur(3), prev(1), prev(2), prev(3),
                  pl.BlockSpec((CONV_K, cwid), lambda i: (0, 0)),
                  pl.BlockSpec((ng, pg, pg), lambda i: (0, 0, 0)),
                  pl.BlockSpec((1, cwid), lambda i: (0, 0))],
        out_specs=[pl.BlockSpec((tm, cwid), lambda i: (i, 0)),
                   pl.BlockSpec((tm, cwid), lambda i: (i, 0))],
        out_shape=[jax.ShapeDtypeStruct((s, cwid), BF16), jax.ShapeDtypeStruct((s, cwid), BF16)],
        scratch_shapes=[pltpu.VMEM((ng, pg, pg), BF16)],
        compiler_params=_params("arbitrary"),
        name="mixer_prep",
    )(proj, proj, proj, proj, proj, proj, proj, conv_w, w_pool_group, pool_scale)


def _merge_epilogue(accs, extra, outs):
    merged = extra[0][...].astype(F32) * accs[0] + extra[1][...].astype(F32) * accs[1]
    outs[0][...] = merged.astype(outs[0].dtype)


def _merge(a, mixed, sg, w_conv_out, w_pool_out, tm=1024, tn=1024):
    s = a.shape[0]
    d = w_conv_out.shape[1]
    nj = d // tn
    return _stream_matmul(
        "branch_merge", [a, mixed], [w_conv_out, w_pool_out], parts=(((0, 0),), ((0, 0),)), tc=tn, n_j=nj, tm=tm,
        epilogue=_merge_epilogue, extra=(sg, sg),
        extra_specs=(pl.BlockSpec((tm, tn), lambda j, i: (i, j)),
                     pl.BlockSpec((tm, tn), lambda j, i: (i, j + nj))),
        out_shapes=(jax.ShapeDtypeStruct((s, d), BF16),),
        out_specs=(pl.BlockSpec((tm, tn), lambda j, i: (i, j)),))[0]


def _resid_epilogue(accs, extra, outs):
    y = accs[0]
    for acc in accs[1:]:
        y = y + acc
    outs[0][...] = extra[0][...] + extra[1][...] * y


def _matmul_resid(name, lhs, w, x, mod, gate_blk, tm, tn):
    s = lhs[0].shape[0]
    d = w.shape[1]
    goff = gate_blk * (d // tn)
    row0, parts = 0, []
    for l in lhs:
        parts.append(((row0, 0),))
        row0 += l.shape[1]
    assert row0 == w.shape[0]
    return _stream_matmul(
        name, lhs, [w] * len(lhs), parts=tuple(parts), tc=tn, n_j=d // tn, tm=tm,
        epilogue=_resid_epilogue, extra=(x, mod),
        extra_specs=(pl.BlockSpec((tm, tn), lambda j, i: (i, j)),
                     pl.BlockSpec((1, tn), lambda j, i: (0, j + goff))),
        out_shapes=(jax.ShapeDtypeStruct((s, d), F32),),
        out_specs=(pl.BlockSpec((tm, tn), lambda j, i: (i, j)),))[0]


def _swiglu_epilogue(accs, extra, outs):
    tc = outs[0].shape[1]
    g = accs[0][:, :tc]
    outs[0][...] = (g * jax.nn.sigmoid(g) * accs[0][:, tc:]).astype(outs[0].dtype)


def _swiglu_up(h, w_gate_up, tm=1024, tc=512):
    s = h.shape[0]
    f = w_gate_up.shape[1] // 2
    bands, col0 = [], 0
    for width, n_j in ((tc, f // tc), (f % tc, 1)):
        if width and n_j:
            bands.append(_stream_matmul(
                "swiglu_up", [h], [w_gate_up], parts=(((0, col0), (0, f + col0)),), tc=width, n_j=n_j, tm=tm,
                epilogue=_swiglu_epilogue,
                out_shapes=(jax.ShapeDtypeStruct((s, width * n_j), BF16),),
                out_specs=(pl.BlockSpec((tm, width), lambda j, i: (i, j)),))[0])
            col0 += width * n_j
    return bands


def kernel(x, c, w_ada, b_ada, norm1_gain, w_in, b_branch_gate, conv_w, w_conv_out, w_pool_group,
           pool_scale, w_pool_out, w_o, norm2_gain, w_gate_up, w_down, final_norm_gain):
    batch, seq, d = x.shape
    depth = w_ada.shape[0]
    cwid = conv_w.shape[2]
    outs = []
    for b in range(batch):
        xb = x[b]
        c_col = c[b].reshape(d, 1)
        for l in range(depth):
            mod = _ada(c_col, w_ada[l], b_ada[l].reshape(1, -1))

            h = _norm_mod(xb, norm1_gain[l].reshape(1, d), mod, 0, 1)
            proj = _in_proj(h, w_in[l], 0, 4 * cwid)
            sg = _in_proj(h, w_in[l], 4 * cwid, 2 * d, bias=b_branch_gate[l].reshape(1, -1))
            a, mixed = _mixer_prep(proj, conv_w[l], w_pool_group[l], pool_scale[l].reshape(1, -1), cwid)
            merged = _merge(a, mixed, sg, w_conv_out[l], w_pool_out[l])
            xb = _matmul_resid("wo_residual", [merged], w_o[l], xb, mod, 2, tm=512, tn=1024)

            h = _norm_mod(xb, norm2_gain[l].reshape(1, d), mod, 3, 4)
            act_bands = _swiglu_up(h, w_gate_up[l])
            xb = _matmul_resid("down_residual", act_bands, w_down[l], xb, mod, 5, tm=512, tn=512)
        outs.append(_norm(xb, final_norm_gain.reshape(1, d)).reshape(1, seq, d))
    return outs[0] if batch == 1 else jnp.concatenate(outs, axis=0)
```

```python
import functools
from typing import NamedTuple

import jax
import jax.numpy as jnp
from jax.experimental import pallas as pl
from jax.experimental.pallas import tpu as pltpu

F32 = jnp.float32
BF16 = jnp.bfloat16

EPS = 1e-6
CONV_K = 3
POOL_WINDOWS = (2, 4, 8, 16)
F32_SUBLANE = 8
BF16_SUBLANE = 16
LANE = 128
POOL_HALO = 16
V7X_VMEM_LIMIT = 56 * 1024 * 1024


def _params(*sem):
    return pltpu.CompilerParams(dimension_semantics=sem, vmem_limit_bytes=V7X_VMEM_LIMIT)


def _ada_kernel(c_ref, w_ref, b_ref, o_ref):
    c = c_ref[...]
    cs = c * jax.nn.sigmoid(c)
    o_ref[...] = jnp.sum(w_ref[...] * cs, axis=0, keepdims=True) + b_ref[...]


def _ada(c_col, w_ada, b_ada, tn=512):
    d, n = w_ada.shape
    return pl.pallas_call(
        _ada_kernel,
        grid=(n // tn,),
        in_specs=[pl.BlockSpec((d, 1), lambda j: (0, 0)),
                  pl.BlockSpec((d, tn), lambda j: (0, j)),
                  pl.BlockSpec((1, tn), lambda j: (0, j))],
        out_specs=pl.BlockSpec((1, tn), lambda j: (0, j)),
        out_shape=jax.ShapeDtypeStruct((1, n), F32),
        compiler_params=_params("arbitrary"),
        name="ada_matvec",
    )(c_col, w_ada, b_ada)


def _norm_mod_kernel(x_ref, g_ref, shift_ref, scale_ref, o_ref):
    x = x_ref[...]
    r = jax.lax.rsqrt(jnp.mean(x * x, axis=-1, keepdims=True) + EPS)
    y = (x * r) * g_ref[...]
    o_ref[...] = (y * (1.0 + scale_ref[...]) + shift_ref[...]).astype(o_ref.dtype)


def _norm_mod(x, gain, mod, shift_blk, scale_blk, tm=512):
    s, d = x.shape
    return pl.pallas_call(
        _norm_mod_kernel,
        grid=(s // tm,),
        in_specs=[pl.BlockSpec((tm, d), lambda i: (i, 0)),
                  pl.BlockSpec((1, d), lambda i: (0, 0)),
                  pl.BlockSpec((1, d), lambda i: (0, shift_blk)),
                  pl.BlockSpec((1, d), lambda i: (0, scale_blk))],
        out_specs=pl.BlockSpec((tm, d), lambda i: (i, 0)),
        out_shape=jax.ShapeDtypeStruct((s, d), BF16),
        compiler_params=_params("arbitrary"),
        name="norm_modulate",
    )(x, gain, mod, mod)


def _norm_kernel(x_ref, g_ref, o_ref):
    x = x_ref[...]
    r = jax.lax.rsqrt(jnp.mean(x * x, axis=-1, keepdims=True) + EPS)
    o_ref[...] = (x * r) * g_ref[...]


def _norm(x, gain, tm=512):
    s, d = x.shape
    return pl.pallas_call(
        _norm_kernel,
        grid=(s // tm,),
        in_specs=[pl.BlockSpec((tm, d), lambda i: (i, 0)),
                  pl.BlockSpec((1, d), lambda i: (0, 0))],
        out_specs=pl.BlockSpec((tm, d), lambda i: (i, 0)),
        out_shape=jax.ShapeDtypeStruct((s, d), F32),
        compiler_params=_params("arbitrary"),
        name="final_norm",
    )(x, gain)


class _TilePos(NamedTuple):
    variant: int
    row0: object
    first: object


def _chunk_copies(w_hbm, stage, sems, parts, tc, ck, tile, chunk, slot):
    copies = []
    for p, (row0, col0) in enumerate(parts):
        col = col0 + tile * tc
        if not isinstance(col, int):
            col = pl.multiple_of(col, LANE)
        row = row0 + chunk * ck
        if not isinstance(row, int):
            row = pl.multiple_of(row, BF16_SUBLANE)
        copies.append(pltpu.make_async_copy(
            w_hbm.at[pl.ds(row, ck), pl.ds(col, tc)],
            stage.at[slot, :, pl.ds(p * tc, tc)],
            sems.at[slot, p]))
    return copies


def _stream_kernel(*refs, n_ops, n_extra, n_out, parts, tc, ck, n_i, n_j, n_split, n_variants, epilogue):
    lhs = refs[:n_ops]
    extra = refs[n_ops:n_ops + n_extra]
    w_hbm = refs[n_ops + n_extra:2 * n_ops + n_extra]
    outs = refs[2 * n_ops + n_extra:2 * n_ops + n_extra + n_out]
    scratch = refs[2 * n_ops + n_extra + n_out:]
    wbf, stage, sems = (scratch[k:3 * n_ops:3] for k in range(3))
    epi_scratch = scratch[3 * n_ops:]
    j = pl.program_id(0)
    i = pl.program_id(1)
    cur = j % 2

    def copies(o, tile, chunk, slot):
        return _chunk_copies(w_hbm[o], stage[o], sems[o], parts[o], tc, ck[o], tile, chunk, slot)

    @pl.when((j == 0) & (i == 0))
    def _load_first_tile():
        for ref in epi_scratch:
            ref[...] = jnp.zeros_like(ref)
        for o in range(n_ops):
            for cp in copies(o, 0, 0, 0):
                cp.start()
            for c in range(n_i):
                if c + 1 < n_i:
                    for cp in copies(o, 0, c + 1, (c + 1) % 2):
                        cp.start()
                for cp in copies(o, 0, c, c % 2):
                    cp.wait()
                wbf[o][0, c * ck[o]:(c + 1) * ck[o], :] = stage[o][c % 2].astype(BF16)
            if n_j > 1:
                for cp in copies(o, 1, 0, 0):
                    cp.start()

    @pl.when(j + 1 < n_j)
    def _stream_next_tile():
        slot = i % 2
        for o in range(n_ops):
            for cp in copies(o, j + 1, i, slot):
                cp.wait()

            @pl.when(i + 1 < n_i)
            def _():
                for cp in copies(o, j + 1, i + 1, 1 - slot):
                    cp.start()

            @pl.when((i + 1 == n_i) & (j + 2 < n_j))
            def _():
                for cp in copies(o, j + 2, 0, 0):
                    cp.start()

    tm = lhs[0].shape[0]
    rows = tm // n_split

    def compute(variant):
        for m in range(n_split):
            sl = pl.ds(m * rows, rows)
            accs = [jnp.dot(lhs[o][sl, :], wbf[o][cur], preferred_element_type=F32) for o in range(n_ops)]
            pos = _TilePos(variant=variant, row0=i * tm + m * rows, first=(i == 0) if m == 0 else False)
            epilogue(accs, [e.at[sl, :] if e.shape[0] == tm else e for e in extra],
                     [out.at[sl, :] for out in outs], epi_scratch, pos)
        for o in range(n_ops):
            row = pl.multiple_of(i * ck[o], BF16_SUBLANE)
            wbf[o][1 - cur, pl.ds(row, ck[o]), :] = stage[o][i % 2].astype(BF16)

    if n_variants == 1:
        compute(0)
    else:
        for v in range(n_variants):
            pl.when(j == v)(functools.partial(compute, v))


def _stream_matmul(name, lhs, weights, parts, tc, n_j, tm, epilogue, extra=(), extra_specs=(),
                   out_shapes=(), out_specs=(), epi_scratch=(), n_split=2, per_tile_variants=False):
    n_ops = len(lhs)
    s = lhs[0].shape[0]
    n_i = s // tm
    assert s % tm == 0 and n_i % 2 == 0, "stage slots alternate per row tile"
    ck = []
    in_specs = [pl.BlockSpec((tm, l.shape[1]), lambda j, i: (i, 0)) for l in lhs]
    in_specs += list(extra_specs)
    in_specs += [pl.BlockSpec(memory_space=pl.ANY) for _ in weights]
    scratch = []
    for o in range(n_ops):
        k = lhs[o].shape[1]
        assert lhs[o].shape[0] == s and k % (n_i * BF16_SUBLANE) == 0
        ck.append(k // n_i)
        tn = len(parts[o]) * tc
        scratch += [pltpu.VMEM((2, k, tn), BF16),
                    pltpu.VMEM((2, ck[o], tn), F32),
                    pltpu.SemaphoreType.DMA((2, len(parts[o])))]
    body = functools.partial(_stream_kernel, n_ops=n_ops, n_extra=len(extra), n_out=len(out_shapes),
                             parts=parts, tc=tc, ck=tuple(ck), n_i=n_i, n_j=n_j, n_split=n_split,
                             n_variants=n_j if per_tile_variants else 1, epilogue=epilogue)
    return pl.pallas_call(
        body,
        grid=(n_j, n_i),
        in_specs=in_specs,
        out_specs=list(out_specs),
        out_shape=list(out_shapes),
        scratch_shapes=scratch + list(epi_scratch),
        compiler_params=_params("arbitrary", "arbitrary"),
        name=name,
    )(*lhs, *extra, *weights)


def _shift_rows(ext, k):
    return pltpu.roll(ext, k, axis=0)


def _with_history(carry_ref, cur, pos):
    hist = carry_ref[...]
    if pos.first is not False:
        hist = jnp.where(pos.first, 0.0, hist)
    carry_ref[...] = cur[-carry_ref.shape[0]:]
    return jnp.concatenate([hist, cur], axis=0)


def _conv_epilogue(accs, extra, outs, scr, pos):
    tc = outs[0].shape[1]
    cw_ref, = extra
    acc = accs[0]
    u = acc[:, :tc] * acc[:, tc:2 * tc]
    hist = scr[0].shape[0]
    ext = _with_history(scr[0], u, pos)
    y = cw_ref[0:1, :] * _shift_rows(ext, 2)[hist:]
    y = y + cw_ref[1:2, :] * _shift_rows(ext, 1)[hist:]
    y = y + cw_ref[2:3, :] * u
    outs[0][...] = (acc[:, 2 * tc:] * y).astype(outs[0].dtype)


def _in_proj_conv(h, w_in, conv_w, cwid, tm=1024, tc=512):
    s = h.shape[0]
    assert conv_w.shape == (CONV_K, cwid)
    return _stream_matmul(
        "in_proj_conv", [h], [w_in], parts=(((0, cwid), (0, 2 * cwid), (0, 0)),), tc=tc, n_j=cwid // tc, tm=tm,
        epilogue=_conv_epilogue, extra=(conv_w,),
        extra_specs=(pl.BlockSpec((CONV_K, tc), lambda j, i: (0, j)),),
        out_shapes=(jax.ShapeDtypeStruct((s, cwid), BF16),),
        out_specs=(pl.BlockSpec((tm, tc), lambda j, i: (i, j)),),
        epi_scratch=(pltpu.VMEM((F32_SUBLANE, tc), F32),))[0]


def _pool_epilogue(accs, extra, outs, scr, pos):
    window = POOL_WINDOWS[pos.variant]
    p = accs[0]
    hist = scr[0].shape[0]
    ext = _with_history(scr[0], p, pos)
    k = 1
    while k < window:
        ext = ext + _shift_rows(ext, k)
        k *= 2
    row = jax.lax.broadcasted_iota(jnp.int32, (p.shape[0], 1), 0) + pos.row0
    count = jnp.minimum(row + 1, window).astype(F32)
    outs[0][...] = (ext[hist:] * (1.0 / count) - p).astype(outs[0].dtype)


def _in_proj_pool(h, w_in, col0, width, tm=1024):
    s = h.shape[0]
    ng = len(POOL_WINDOWS)
    assert all(w & (w - 1) == 0 and w <= POOL_HALO for w in POOL_WINDOWS)
    tc = width // ng
    return _stream_matmul(
        "in_proj_pool", [h], [w_in], parts=(((0, col0),),), tc=tc, n_j=ng, tm=tm,
        epilogue=_pool_epilogue,
        out_shapes=(jax.ShapeDtypeStruct((s, width), BF16),),
        out_specs=(pl.BlockSpec((tm, tc), lambda j, i: (i, j)),),
        epi_scratch=(pltpu.VMEM((POOL_HALO, tc), F32),), n_split=2, per_tile_variants=True)[0]


def _gate_epilogue(accs, extra, outs, scr, pos):
    outs[0][...] = jax.nn.sigmoid(accs[0] + extra[0][...]).astype(outs[0].dtype)


def _in_proj_gate(h, w_in, col0, bias, tm=1024, tn=1024):
    s = h.shape[0]
    ncols = bias.shape[1]
    return _stream_matmul(
        "in_proj_gate", [h], [w_in], parts=(((0, col0),),), tc=tn, n_j=ncols // tn, tm=tm,
        epilogue=_gate_epilogue, extra=(bias,),
        extra_specs=(pl.BlockSpec((1, tn), lambda j, i: (0, j)),),
        out_shapes=(jax.ShapeDtypeStruct((s, ncols), BF16),),
        out_specs=(pl.BlockSpec((tm, tn), lambda j, i: (i, j)),))[0]


def _fold_kernel(wg_ref, ps_ref, wpo_ref, o_ref):
    lhs = (wg_ref[0] * ps_ref[...]).astype(BF16)
    o_ref[...] = jnp.dot(lhs, wpo_ref[...].astype(BF16), preferred_element_type=F32)


def _fold_pool_weights(w_pool_group, pool_scale, w_pool_out, tn=1024):
    ng, pg, _ = w_pool_group.shape
    d = w_pool_out.shape[1]
    return pl.pallas_call(
        _fold_kernel,
        grid=(ng, d // tn),
        in_specs=[pl.BlockSpec((1, pg, pg), lambda g, n: (g, 0, 0)),
                  pl.BlockSpec((1, pg), lambda g, n: (0, g)),
                  pl.BlockSpec((pg, tn), lambda g, n: (g, n))],
        out_specs=pl.BlockSpec((pg, tn), lambda g, n: (g, n)),
        out_shape=jax.ShapeDtypeStruct((ng * pg, d), F32),
        compiler_params=_params("arbitrary", "arbitrary"),
        name="fold_pool_weights",
    )(w_pool_group, pool_scale, w_pool_out)


def _merge_epilogue(accs, extra, outs, scr, pos):
    merged = extra[0][...].astype(F32) * accs[0] + extra[1][...].astype(F32) * accs[1]
    outs[0][...] = merged.astype(outs[0].dtype)


def _merge(a, pooled, sg, w_conv_out, w_pool_fold, tm=1024, tn=1024):
    s = a.shape[0]
    d = w_conv_out.shape[1]
    nj = d // tn
    return _stream_matmul(
        "branch_merge", [a, pooled], [w_conv_out, w_pool_fold], parts=(((0, 0),), ((0, 0),)), tc=tn, n_j=nj, tm=tm,
        epilogue=_merge_epilogue, extra=(sg, sg),
        extra_specs=(pl.BlockSpec((tm, tn), lambda j, i: (i, j)),
                     pl.BlockSpec((tm, tn), lambda j, i: (i, j + nj))),
        out_shapes=(jax.ShapeDtypeStruct((s, d), BF16),),
        out_specs=(pl.BlockSpec((tm, tn), lambda j, i: (i, j)),))[0]


def _resid_epilogue(accs, extra, outs, scr, pos):
    y = accs[0]
    for acc in accs[1:]:
        y = y + acc
    outs[0][...] = extra[0][...] + extra[1][...] * y


def _matmul_resid(name, lhs, w, x, mod, gate_blk, tm, tn):
    s = lhs[0].shape[0]
    d = w.shape[1]
    goff = gate_blk * (d // tn)
    row0, parts = 0, []
    for l in lhs:
        parts.append(((row0, 0),))
        row0 += l.shape[1]
    assert row0 == w.shape[0]
    return _stream_matmul(
        name, lhs, [w] * len(lhs), parts=tuple(parts), tc=tn, n_j=d // tn, tm=tm,
        epilogue=_resid_epilogue, extra=(x, mod),
        extra_specs=(pl.BlockSpec((tm, tn), lambda j, i: (i, j)),
                     pl.BlockSpec((1, tn), lambda j, i: (0, j + goff))),
        out_shapes=(jax.ShapeDtypeStruct((s, d), F32),),
        out_specs=(pl.BlockSpec((tm, tn), lambda j, i: (i, j)),))[0]


def _swiglu_epilogue(accs, extra, outs, scr, pos):
    tc = outs[0].shape[1]
    g = accs[0][:, :tc]
    outs[0][...] = (g * jax.nn.sigmoid(g) * accs[0][:, tc:]).astype(outs[0].dtype)


def _swiglu_up(h, w_gate_up, tm=1024, tc=512):
    s = h.shape[0]
    f = w_gate_up.shape[1] // 2
    bands, col0 = [], 0
    for width, n_j in ((tc, f // tc), (f % tc, 1)):
        if width and n_j:
            bands.append(_stream_matmul(
                "swiglu_up", [h], [w_gate_up], parts=(((0, col0), (0, f + col0)),), tc=width, n_j=n_j, tm=tm,
                epilogue=_swiglu_epilogue,
                out_shapes=(jax.ShapeDtypeStruct((s, width * n_j), BF16),),
                out_specs=(pl.BlockSpec((tm, width), lambda j, i: (i, j)),))[0])
            col0 += width * n_j
    return bands


def kernel(x, c, w_ada, b_ada, norm1_gain, w_in, b_branch_gate, conv_w, w_conv_out, w_pool_group,
           pool_scale, w_pool_out, w_o, norm2_gain, w_gate_up, w_down, final_norm_gain):
    batch, seq, d = x.shape
    depth = w_ada.shape[0]
    cwid = conv_w.shape[2]
    pwid = pool_scale.shape[1]
    outs = []
    for b in range(batch):
        xb = x[b]
        c_col = c[b].reshape(d, 1)
        for l in range(depth):
            mod = _ada(c_col, w_ada[l], b_ada[l].reshape(1, -1))

            h = _norm_mod(xb, norm1_gain[l].reshape(1, d), mod, 0, 1)
            a = _in_proj_conv(h, w_in[l], conv_w[l], cwid)
            pooled = _in_proj_pool(h, w_in[l], 3 * cwid, pwid)
            sg = _in_proj_gate(h, w_in[l], 3 * cwid + pwid, b_branch_gate[l].reshape(1, -1))
            w_pool_fold = _fold_pool_weights(w_pool_group[l], pool_scale[l].reshape(1, -1), w_pool_out[l])
            merged = _merge(a, pooled, sg, w_conv_out[l], w_pool_fold)
            xb = _matmul_resid("wo_residual", [merged], w_o[l], xb, mod, 2, tm=512, tn=1024)

            h = _norm_mod(xb, norm2_gain[l].reshape(1, d), mod, 3, 4)
            act_bands = _swiglu_up(h, w_gate_up[l])
            xb = _matmul_resid("down_residual", act_bands, w_down[l], xb, mod, 5, tm=512, tn=512)
        outs.append(_norm(xb, final_norm_gain.reshape(1, d)).reshape(1, seq, d))
    return outs[0] if batch == 1 else jnp.concatenate(outs, axis=0)
```

```python
import functools
from typing import NamedTuple

import jax
import jax.numpy as jnp
from jax.experimental import pallas as pl
from jax.experimental.pallas import tpu as pltpu

F32 = jnp.float32
BF16 = jnp.bfloat16

EPS = 1e-6
CONV_K = 3
POOL_WINDOWS = (2, 4, 8, 16)
F32_SUBLANE = 8
BF16_SUBLANE = 16
LANE = 128
POOL_HALO = 16
V7X_VMEM_LIMIT = 56 * 1024 * 1024


def _params(*sem):
    return pltpu.CompilerParams(dimension_semantics=sem, vmem_limit_bytes=V7X_VMEM_LIMIT)


def _ada_kernel(c_ref, w_ref, b_ref, o_ref):
    c = c_ref[...]
    cs = c * jax.nn.sigmoid(c)
    o_ref[...] = jnp.sum(w_ref[...] * cs, axis=0, keepdims=True) + b_ref[...]


def _ada(c_col, w_ada, b_ada, tn=512):
    d, n = w_ada.shape
    return pl.pallas_call(
        _ada_kernel,
        grid=(n // tn,),
        in_specs=[pl.BlockSpec((d, 1), lambda j: (0, 0)),
                  pl.BlockSpec((d, tn), lambda j: (0, j)),
                  pl.BlockSpec((1, tn), lambda j: (0, j))],
        out_specs=pl.BlockSpec((1, tn), lambda j: (0, j)),
        out_shape=jax.ShapeDtypeStruct((1, n), F32),
        compiler_params=_params("arbitrary"),
        name="ada_matvec",
    )(c_col, w_ada, b_ada)


def _norm_mod_kernel(x_ref, g_ref, shift_ref, scale_ref, o_ref):
    x = x_ref[...]
    r = jax.lax.rsqrt(jnp.mean(x * x, axis=-1, keepdims=True) + EPS)
    y = (x * r) * g_ref[...]
    o_ref[...] = (y * (1.0 + scale_ref[...]) + shift_ref[...]).astype(o_ref.dtype)


def _norm_mod(x, gain, mod, shift_blk, scale_blk, tm=512):
    s, d = x.shape
    return pl.pallas_call(
        _norm_mod_kernel,
        grid=(s // tm,),
        in_specs=[pl.BlockSpec((tm, d), lambda i: (i, 0)),
                  pl.BlockSpec((1, d), lambda i: (0, 0)),
                  pl.BlockSpec((1, d), lambda i: (0, shift_blk)),
                  pl.BlockSpec((1, d), lambda i: (0, scale_blk))],
        out_specs=pl.BlockSpec((tm, d), lambda i: (i, 0)),
        out_shape=jax.ShapeDtypeStruct((s, d), BF16),
        compiler_params=_params("arbitrary"),
        name="norm_modulate",
    )(x, gain, mod, mod)


def _norm_kernel(x_ref, g_ref, o_ref):
    x = x_ref[...]
    r = jax.lax.rsqrt(jnp.mean(x * x, axis=-1, keepdims=True) + EPS)
    o_ref[...] = (x * r) * g_ref[...]


def _norm(x, gain, tm=512):
    s, d = x.shape
    return pl.pallas_call(
        _norm_kernel,
        grid=(s // tm,),
        in_specs=[pl.BlockSpec((tm, d), lambda i: (i, 0)),
                  pl.BlockSpec((1, d), lambda i: (0, 0))],
        out_specs=pl.BlockSpec((tm, d), lambda i: (i, 0)),
        out_shape=jax.ShapeDtypeStruct((s, d), F32),
        compiler_params=_params("arbitrary"),
        name="final_norm",
    )(x, gain)


class _TilePos(NamedTuple):
    variant: int
    row0: object
    first: object


def _chunk_copies(w_hbm, stage, sems, parts, tc, ck, tile, chunk, slot):
    copies = []
    for p, (row0, col0) in enumerate(parts):
        col = col0 + tile * tc
        if not isinstance(col, int):
            col = pl.multiple_of(col, LANE)
        row = row0 + chunk * ck
        if not isinstance(row, int):
            row = pl.multiple_of(row, BF16_SUBLANE)
        copies.append(pltpu.make_async_copy(
            w_hbm.at[pl.ds(row, ck), pl.ds(col, tc)],
            stage.at[slot, :, pl.ds(p * tc, tc)],
            sems.at[slot, p]))
    return copies


def _stream_kernel(*refs, n_ops, n_extra, n_out, parts, tc, ck, n_i, n_j, n_split, n_variants, epilogue):
    lhs = refs[:n_ops]
    extra = refs[n_ops:n_ops + n_extra]
    w_hbm = refs[n_ops + n_extra:2 * n_ops + n_extra]
    outs = refs[2 * n_ops + n_extra:2 * n_ops + n_extra + n_out]
    scratch = refs[2 * n_ops + n_extra + n_out:]
    wbf, stage, sems = (scratch[k:3 * n_ops:3] for k in range(3))
    epi_scratch = scratch[3 * n_ops:]
    j = pl.program_id(0)
    i = pl.program_id(1)
    cur = j % 2

    def copies(o, tile, chunk, slot):
        return _chunk_copies(w_hbm[o], stage[o], sems[o], parts[o], tc, ck[o], tile, chunk, slot)

    @pl.when((j == 0) & (i == 0))
    def _load_first_tile():
        for ref in epi_scratch:
            ref[...] = jnp.zeros_like(ref)
        for o in range(n_ops):
            for cp in copies(o, 0, 0, 0):
                cp.start()
            for c in range(n_i):
                if c + 1 < n_i:
                    for cp in copies(o, 0, c + 1, (c + 1) % 2):
                        cp.start()
                for cp in copies(o, 0, c, c % 2):
                    cp.wait()
                wbf[o][0, c * ck[o]:(c + 1) * ck[o], :] = stage[o][c % 2].astype(BF16)
            if n_j > 1:
                for cp in copies(o, 1, 0, 0):
                    cp.start()

    @pl.when(j + 1 < n_j)
    def _stream_next_tile():
        slot = i % 2
        for o in range(n_ops):
            for cp in copies(o, j + 1, i, slot):
                cp.wait()

            @pl.when(i + 1 < n_i)
            def _():
                for cp in copies(o, j + 1, i + 1, 1 - slot):
                    cp.start()

            @pl.when((i + 1 == n_i) & (j + 2 < n_j))
            def _():
                for cp in copies(o, j + 2, 0, 0):
                    cp.start()

            row = pl.multiple_of(i * ck[o], BF16_SUBLANE)
            wbf[o][1 - cur, pl.ds(row, ck[o]), :] = stage[o][slot].astype(BF16)

    tm = lhs[0].shape[0]
    rows = tm // n_split

    def compute(variant):
        for m in range(n_split):
            sl = pl.ds(m * rows, rows)
            accs = [jnp.dot(lhs[o][sl, :], wbf[o][cur], preferred_element_type=F32) for o in range(n_ops)]
            pos = _TilePos(variant=variant, row0=i * tm + m * rows, first=(i == 0) if m == 0 else False)
            epilogue(accs, [e.at[sl, :] if e.shape[0] == tm else e for e in extra],
                     [out.at[sl, :] for out in outs], epi_scratch, pos)

    if n_variants == 1:
        compute(0)
    else:
        for v in range(n_variants):
            pl.when(j == v)(functools.partial(compute, v))


def _stream_matmul(name, lhs, weights, parts, tc, n_j, tm, epilogue, extra=(), extra_specs=(),
                   out_shapes=(), out_specs=(), epi_scratch=(), n_split=1, per_tile_variants=False):
    n_ops = len(lhs)
    s = lhs[0].shape[0]
    n_i = s // tm
    assert s % tm == 0 and n_i % 2 == 0, "stage slots alternate per row tile"
    ck = []
    in_specs = [pl.BlockSpec((tm, l.shape[1]), lambda j, i: (i, 0)) for l in lhs]
    in_specs += list(extra_specs)
    in_specs += [pl.BlockSpec(memory_space=pl.ANY) for _ in weights]
    scratch = []
    for o in range(n_ops):
        k = lhs[o].shape[1]
        assert lhs[o].shape[0] == s and k % (n_i * BF16_SUBLANE) == 0
        ck.append(k // n_i)
        tn = len(parts[o]) * tc
        scratch += [pltpu.VMEM((2, k, tn), BF16),
                    pltpu.VMEM((2, ck[o], tn), F32),
                    pltpu.SemaphoreType.DMA((2, len(parts[o])))]
    body = functools.partial(_stream_kernel, n_ops=n_ops, n_extra=len(extra), n_out=len(out_shapes),
                             parts=parts, tc=tc, ck=tuple(ck), n_i=n_i, n_j=n_j, n_split=n_split,
                             n_variants=n_j if per_tile_variants else 1, epilogue=epilogue)
    return pl.pallas_call(
        body,
        grid=(n_j, n_i),
        in_specs=in_specs,
        out_specs=list(out_specs),
        out_shape=list(out_shapes),
        scratch_shapes=scratch + list(epi_scratch),
        compiler_params=_params("arbitrary", "arbitrary"),
        name=name,
    )(*lhs, *extra, *weights)


def _shift_rows(ext, k):
    return pltpu.roll(ext, k, axis=0)


def _with_history(carry_ref, cur, pos):
    hist = carry_ref[...]
    if pos.first is not False:
        hist = jnp.where(pos.first, 0.0, hist)
    carry_ref[...] = cur[-carry_ref.shape[0]:]
    return jnp.concatenate([hist, cur], axis=0)


def _conv_epilogue(accs, extra, outs, scr, pos):
    tc = outs[0].shape[1]
    cw_ref, = extra
    acc = accs[0]
    u = acc[:, :tc] * acc[:, tc:2 * tc]
    hist = scr[0].shape[0]
    ext = _with_history(scr[0], u, pos)
    y = cw_ref[0:1, :] * _shift_rows(ext, 2)[hist:]
    y = y + cw_ref[1:2, :] * _shift_rows(ext, 1)[hist:]
    y = y + cw_ref[2:3, :] * u
    outs[0][...] = (acc[:, 2 * tc:] * y).astype(outs[0].dtype)


def _in_proj_conv(h, w_in, conv_w, cwid, tm=1024, tc=512):
    s = h.shape[0]
    assert conv_w.shape == (CONV_K, cwid)
    return _stream_matmul(
        "in_proj_conv", [h], [w_in], parts=(((0, cwid), (0, 2 * cwid), (0, 0)),), tc=tc, n_j=cwid // tc, tm=tm,
        epilogue=_conv_epilogue, extra=(conv_w,),
        extra_specs=(pl.BlockSpec((CONV_K, tc), lambda j, i: (0, j)),),
        out_shapes=(jax.ShapeDtypeStruct((s, cwid), BF16),),
        out_specs=(pl.BlockSpec((tm, tc), lambda j, i: (i, j)),),
        epi_scratch=(pltpu.VMEM((F32_SUBLANE, tc), F32),), n_split=2)[0]


def _pool_epilogue(accs, extra, outs, scr, pos):
    window = POOL_WINDOWS[pos.variant]
    p = accs[0]
    hist = scr[0].shape[0]
    ext = _with_history(scr[0], p, pos)
    k = 1
    while k < window:
        ext = ext + _shift_rows(ext, k)
        k *= 2
    row = jax.lax.broadcasted_iota(jnp.int32, (p.shape[0], 1), 0) + pos.row0
    count = jnp.minimum(row + 1, window).astype(F32)
    outs[0][...] = (ext[hist:] * (1.0 / count) - p).astype(outs[0].dtype)


def _in_proj_pool(h, w_in, col0, width, tm=1024):
    s = h.shape[0]
    ng = len(POOL_WINDOWS)
    assert all(w & (w - 1) == 0 and w <= POOL_HALO for w in POOL_WINDOWS)
    tc = width // ng
    return _stream_matmul(
        "in_proj_pool", [h], [w_in], parts=(((0, col0),),), tc=tc, n_j=ng, tm=tm,
        epilogue=_pool_epilogue,
        out_shapes=(jax.ShapeDtypeStruct((s, width), BF16),),
        out_specs=(pl.BlockSpec((tm, tc), lambda j, i: (i, j)),),
        epi_scratch=(pltpu.VMEM((POOL_HALO, tc), F32),), n_split=2, per_tile_variants=True)[0]


def _sigmoid(x):
    return 0.5 * jnp.tanh(0.5 * x) + 0.5


def _gate_epilogue(accs, extra, outs, scr, pos):
    outs[0][...] = _sigmoid(accs[0] + extra[0][...]).astype(outs[0].dtype)


def _in_proj_gate(h, w_in, col0, bias, tm=1024, tn=1024):
    s = h.shape[0]
    ncols = bias.shape[1]
    return _stream_matmul(
        "in_proj_gate", [h], [w_in], parts=(((0, col0),),), tc=tn, n_j=ncols // tn, tm=tm,
        epilogue=_gate_epilogue, extra=(bias,),
        extra_specs=(pl.BlockSpec((1, tn), lambda j, i: (0, j)),),
        out_shapes=(jax.ShapeDtypeStruct((s, ncols), BF16),),
        out_specs=(pl.BlockSpec((tm, tn), lambda j, i: (i, j)),))[0]


def _fold_kernel(wg_ref, ps_ref, wpo_ref, o_ref):
    lhs = (wg_ref[0] * ps_ref[...]).astype(BF16)
    o_ref[...] = jnp.dot(lhs, wpo_ref[...].astype(BF16), preferred_element_type=F32)


def _fold_pool_weights(w_pool_group, pool_scale, w_pool_out, tn=2048):
    ng, pg, _ = w_pool_group.shape
    d = w_pool_out.shape[1]
    return pl.pallas_call(
        _fold_kernel,
        grid=(ng, d // tn),
        in_specs=[pl.BlockSpec((1, pg, pg), lambda g, n: (g, 0, 0)),
                  pl.BlockSpec((1, pg), lambda g, n: (0, g)),
                  pl.BlockSpec((pg, tn), lambda g, n: (g, n))],
        out_specs=pl.BlockSpec((pg, tn), lambda g, n: (g, n)),
        out_shape=jax.ShapeDtypeStruct((ng * pg, d), F32),
        compiler_params=_params("arbitrary", "arbitrary"),
        name="fold_pool_weights",
    )(w_pool_group, pool_scale, w_pool_out)


def _merge_epilogue(accs, extra, outs, scr, pos):
    merged = extra[0][...].astype(F32) * accs[0] + extra[1][...].astype(F32) * accs[1]
    outs[0][...] = merged.astype(outs[0].dtype)


def _merge(a, pooled, sg, w_conv_out, w_pool_fold, tm=1024, tn=1024):
    s = a.shape[0]
    d = w_conv_out.shape[1]
    nj = d // tn
    return _stream_matmul(
        "branch_merge", [a, pooled], [w_conv_out, w_pool_fold], parts=(((0, 0),), ((0, 0),)), tc=tn, n_j=nj, tm=tm,
        epilogue=_merge_epilogue, extra=(sg, sg),
        extra_specs=(pl.BlockSpec((tm, tn), lambda j, i: (i, j)),
                     pl.BlockSpec((tm, tn), lambda j, i: (i, j + nj))),
        out_shapes=(jax.ShapeDtypeStruct((s, d), BF16),),
        out_specs=(pl.BlockSpec((tm, tn), lambda j, i: (i, j)),))[0]


def _resid_epilogue(accs, extra, outs, scr, pos):
    y = accs[0]
    for acc in accs[1:]:
        y = y + acc
    outs[0][...] = extra[0][...] + extra[1][...] * y


def _matmul_resid(name, lhs, w, x, mod, gate_blk, tm, tn):
    s = lhs[0].shape[0]
    d = w.shape[1]
    goff = gate_blk * (d // tn)
    row0, parts = 0, []
    for l in lhs:
        parts.append(((row0, 0),))
        row0 += l.shape[1]
    assert row0 == w.shape[0]
    return _stream_matmul(
        name, lhs, [w] * len(lhs), parts=tuple(parts), tc=tn, n_j=d // tn, tm=tm,
        epilogue=_resid_epilogue, extra=(x, mod),
        extra_specs=(pl.BlockSpec((tm, tn), lambda j, i: (i, j)),
                     pl.BlockSpec((1, tn), lambda j, i: (0, j + goff))),
        out_shapes=(jax.ShapeDtypeStruct((s, d), F32),),
        out_specs=(pl.BlockSpec((tm, tn), lambda j, i: (i, j)),))[0]


def _swiglu_epilogue(accs, extra, outs, scr, pos):
    tc = outs[0].shape[1]
    g = accs[0][:, :tc]
    outs[0][...] = (g * _sigmoid(g) * accs[0][:, tc:]).astype(outs[0].dtype)


def _swiglu_up(h, w_gate_up, tm=1024, tc=512):
    s = h.shape[0]
    f = w_gate_up.shape[1] // 2
    bands, col0 = [], 0
    for width, n_j in ((tc, f // tc), (f % tc, 1)):
        if width and n_j:
            bands.append(_stream_matmul(
                "swiglu_up", [h], [w_gate_up], parts=(((0, col0), (0, f + col0)),), tc=width, n_j=n_j, tm=tm,
                epilogue=_swiglu_epilogue,
                out_shapes=(jax.ShapeDtypeStruct((s, width * n_j), BF16),),
                out_specs=(pl.BlockSpec((tm, width), lambda j, i: (i, j)),))[0])
            col0 += width * n_j
    return bands


def kernel(x, c, w_ada, b_ada, norm1_gain, w_in, b_branch_gate, conv_w, w_conv_out, w_pool_group,
           pool_scale, w_pool_out, w_o, norm2_gain, w_gate_up, w_down, final_norm_gain):
    batch, seq, d = x.shape
    depth = w_ada.shape[0]
    cwid = conv_w.shape[2]
    pwid = pool_scale.shape[1]
    outs = []
    for b in range(batch):
        xb = x[b]
        c_col = c[b].reshape(d, 1)
        for l in range(depth):
            mod = _ada(c_col, w_ada[l], b_ada[l].reshape(1, -1))

            h = _norm_mod(xb, norm1_gain[l].reshape(1, d), mod, 0, 1)
            a = _in_proj_conv(h, w_in[l], conv_w[l], cwid)
            pooled = _in_proj_pool(h, w_in[l], 3 * cwid, pwid)
            sg = _in_proj_gate(h, w_in[l], 3 * cwid + pwid, b_branch_gate[l].reshape(1, -1))
            w_pool_fold = _fold_pool_weights(w_pool_group[l], pool_scale[l].reshape(1, -1), w_pool_out[l])
            merged = _merge(a, pooled, sg, w_conv_out[l], w_pool_fold)
            xb = _matmul_resid("wo_residual", [merged], w_o[l], xb, mod, 2, tm=512, tn=1024)

            h = _norm_mod(xb, norm2_gain[l].reshape(1, d), mod, 3, 4)
            act_bands = _swiglu_up(h, w_gate_up[l])
            xb = _matmul_resid("down_residual", act_bands, w_down[l], xb, mod, 5, tm=512, tn=512)
        outs.append(_norm(xb, final_norm_gain.reshape(1, d)).reshape(1, seq, d))
    return outs[0] if batch == 1 else jnp.concatenate(outs, axis=0)
```

```python
import functools
from typing import NamedTuple

import jax
import jax.numpy as jnp
from jax.experimental import pallas as pl
from jax.experimental.pallas import tpu as pltpu

F32 = jnp.float32
BF16 = jnp.bfloat16

EPS = 1e-6
CONV_K = 3
POOL_WINDOWS = (2, 4, 8, 16)
F32_SUBLANE = 8
BF16_SUBLANE = 16
LANE = 128
POOL_HALO = 16
V7X_VMEM_LIMIT = 60 * 1024 * 1024


def _params(*sem):
    return pltpu.CompilerParams(dimension_semantics=sem, vmem_limit_bytes=V7X_VMEM_LIMIT)


def _ada_kernel(c_ref, w_ref, b_ref, o_ref):
    c = c_ref[...]
    cs = c * jax.nn.sigmoid(c)
    o_ref[...] = jnp.sum(w_ref[...] * cs, axis=0, keepdims=True) + b_ref[...]


def _ada(c_col, w_ada, b_ada, n, tn=512):
    d = w_ada.shape[0]
    return pl.pallas_call(
        _ada_kernel,
        grid=(n // tn,),
        in_specs=[pl.BlockSpec((d, 1), lambda j: (0, 0)),
                  pl.BlockSpec((d, tn), lambda j: (0, j)),
                  pl.BlockSpec((1, tn), lambda j: (0, j))],
        out_specs=pl.BlockSpec((1, tn), lambda j: (0, j)),
        out_shape=jax.ShapeDtypeStruct((1, n), F32),
        compiler_params=_params("arbitrary"),
        name="ada_matvec",
    )(c_col, w_ada, b_ada)


def _norm_mod_kernel(x_ref, g_ref, shift_ref, scale_ref, o_ref):
    x = x_ref[...]
    r = jax.lax.rsqrt(jnp.mean(x * x, axis=-1, keepdims=True) + EPS)
    y = (x * r) * g_ref[...]
    o_ref[...] = (y * (1.0 + scale_ref[...]) + shift_ref[...]).astype(o_ref.dtype)


def _norm_mod(x, gain, mod, shift_blk, scale_blk, tm=512):
    s, d = x.shape
    return pl.pallas_call(
        _norm_mod_kernel,
        grid=(s // tm,),
        in_specs=[pl.BlockSpec((tm, d), lambda i: (i, 0)),
                  pl.BlockSpec((1, d), lambda i: (0, 0)),
                  pl.BlockSpec((1, d), lambda i: (0, shift_blk)),
                  pl.BlockSpec((1, d), lambda i: (0, scale_blk))],
        out_specs=pl.BlockSpec((tm, d), lambda i: (i, 0)),
        out_shape=jax.ShapeDtypeStruct((s, d), BF16),
        compiler_params=_params("arbitrary"),
        name="norm_modulate",
    )(x, gain, mod, mod)


def _norm_kernel(x_ref, g_ref, o_ref):
    x = x_ref[...]
    r = jax.lax.rsqrt(jnp.mean(x * x, axis=-1, keepdims=True) + EPS)
    o_ref[...] = (x * r) * g_ref[...]


def _norm(x, gain, tm=512):
    s, d = x.shape
    return pl.pallas_call(
        _norm_kernel,
        grid=(s // tm,),
        in_specs=[pl.BlockSpec((tm, d), lambda i: (i, 0)),
                  pl.BlockSpec((1, d), lambda i: (0, 0))],
        out_specs=pl.BlockSpec((tm, d), lambda i: (i, 0)),
        out_shape=jax.ShapeDtypeStruct((s, d), F32),
        compiler_params=_params("arbitrary"),
        name="final_norm",
    )(x, gain)


class _TilePos(NamedTuple):
    variant: int
    row0: object
    first: object


def _chunk_copies(w_hbm, stage, sems, parts, tc, ck, tile, chunk, slot):
    copies = []
    for p, (row0, col0) in enumerate(parts):
        col = col0 + tile * tc
        if not isinstance(col, int):
            col = pl.multiple_of(col, LANE)
        row = row0 + chunk * ck
        if not isinstance(row, int):
            row = pl.multiple_of(row, BF16_SUBLANE)
        copies.append(pltpu.make_async_copy(
            w_hbm.at[pl.ds(row, ck), pl.ds(col, tc)],
            stage.at[slot, :, pl.ds(p * tc, tc)],
            sems.at[slot, p]))
    return copies


def _stream_kernel(*refs, n_ops, n_extra, n_out, parts, tc, ck, n_i, n_j, n_split, n_variants, epilogue,
                   side_task):
    lhs = refs[:n_ops]
    extra = refs[n_ops:n_ops + n_extra]
    w_hbm = refs[n_ops + n_extra:2 * n_ops + n_extra]
    outs = refs[2 * n_ops + n_extra:2 * n_ops + n_extra + n_out]
    scratch = refs[2 * n_ops + n_extra + n_out:]
    wbf, stage, sems = (scratch[k:3 * n_ops:3] for k in range(3))
    epi_scratch = scratch[3 * n_ops:]
    j = pl.program_id(0)
    i = pl.program_id(1)
    cur = j % 2

    def copies(o, tile, chunk, slot):
        return _chunk_copies(w_hbm[o], stage[o], sems[o], parts[o], tc, ck[o], tile, chunk, slot)

    @pl.when((j == 0) & (i == 0))
    def _load_first_tile():
        for ref in epi_scratch:
            ref[...] = jnp.zeros_like(ref)
        for o in range(n_ops):
            for cp in copies(o, 0, 0, 0):
                cp.start()
            for c in range(n_i):
                if c + 1 < n_i:
                    for cp in copies(o, 0, c + 1, (c + 1) % 2):
                        cp.start()
                for cp in copies(o, 0, c, c % 2):
                    cp.wait()
                wbf[o][0, c * ck[o]:(c + 1) * ck[o], :] = stage[o][c % 2].astype(BF16)
            if n_j > 1:
                for cp in copies(o, 1, 0, 0):
                    cp.start()

    if side_task is not None:
        side_task(extra, outs, epi_scratch)

    @pl.when(j + 1 < n_j)
    def _stream_next_tile():
        slot = i % 2
        for o in range(n_ops):
            for cp in copies(o, j + 1, i, slot):
                cp.wait()

            @pl.when(i + 1 < n_i)
            def _():
                for cp in copies(o, j + 1, i + 1, 1 - slot):
                    cp.start()

            @pl.when((i + 1 == n_i) & (j + 2 < n_j))
            def _():
                for cp in copies(o, j + 2, 0, 0):
                    cp.start()

            row = pl.multiple_of(i * ck[o], BF16_SUBLANE)
            wbf[o][1 - cur, pl.ds(row, ck[o]), :] = stage[o][slot].astype(BF16)

    tm = lhs[0].shape[0]
    rows = tm // n_split

    def compute(variant):
        for m in range(n_split):
            sl = pl.ds(m * rows, rows)
            accs = [jnp.dot(lhs[o][sl, :], wbf[o][cur], preferred_element_type=F32) for o in range(n_ops)]
            pos = _TilePos(variant=variant, row0=i * tm + m * rows, first=(i == 0) if m == 0 else False)
            epilogue(accs, [e.at[sl, :] if e.shape[0] == tm else e for e in extra],
                     [out.at[sl, :] if out.shape[0] == tm else out for out in outs], epi_scratch, pos)

    if n_variants == 1:
        compute(0)
    else:
        for v in range(n_variants):
            pl.when(j == v)(functools.partial(compute, v))


def _stream_matmul(name, lhs, weights, parts, tc, n_j, tm, epilogue, extra=(), extra_specs=(),
                   out_shapes=(), out_specs=(), epi_scratch=(), n_split=1, per_tile_variants=False,
                   side_task=None):
    n_ops = len(lhs)
    s = lhs[0].shape[0]
    n_i = s // tm
    assert s % tm == 0 and n_i % 2 == 0, "stage slots alternate per row tile"
    ck = []
    in_specs = [pl.BlockSpec((tm, l.shape[1]), lambda j, i: (i, 0)) for l in lhs]
    in_specs += list(extra_specs)
    in_specs += [pl.BlockSpec(memory_space=pl.ANY) for _ in weights]
    scratch = []
    for o in range(n_ops):
        k = lhs[o].shape[1]
        assert lhs[o].shape[0] == s and k % (n_i * BF16_SUBLANE) == 0
        ck.append(k // n_i)
        tn = len(parts[o]) * tc
        scratch += [pltpu.VMEM((2, k, tn), BF16),
                    pltpu.VMEM((2, ck[o], tn), F32),
                    pltpu.SemaphoreType.DMA((2, len(parts[o])))]
    body = functools.partial(_stream_kernel, n_ops=n_ops, n_extra=len(extra), n_out=len(out_shapes),
                             parts=parts, tc=tc, ck=tuple(ck), n_i=n_i, n_j=n_j, n_split=n_split,
                             n_variants=n_j if per_tile_variants else 1, epilogue=epilogue, side_task=side_task)
    return pl.pallas_call(
        body,
        grid=(n_j, n_i),
        in_specs=in_specs,
        out_specs=list(out_specs),
        out_shape=list(out_shapes),
        scratch_shapes=scratch + list(epi_scratch),
        compiler_params=_params("arbitrary", "arbitrary"),
        name=name,
    )(*lhs, *extra, *weights)


def _shift_rows(ext, k):
    return pltpu.roll(ext, k, axis=0)


def _with_history(carry_ref, cur, pos):
    hist = carry_ref[...]
    if pos.first is not False:
        hist = jnp.where(pos.first, 0.0, hist)
    carry_ref[...] = cur[-carry_ref.shape[0]:]
    return jnp.concatenate([hist, cur], axis=0)


def _conv_epilogue(accs, extra, outs, scr, pos):
    tc = outs[0].shape[1]
    cw_ref, = extra
    acc = accs[0]
    u = acc[:, :tc] * acc[:, tc:2 * tc]
    hist = scr[0].shape[0]
    ext = _with_history(scr[0], u, pos)
    y = cw_ref[0:1, :] * _shift_rows(ext, 2)[hist:]
    y = y + cw_ref[1:2, :] * _shift_rows(ext, 1)[hist:]
    y = y + cw_ref[2:3, :] * u
    outs[0][...] = (acc[:, 2 * tc:] * y).astype(outs[0].dtype)


def _in_proj_conv(h, w_in, conv_w, cwid, tm=1024, tc=512):
    s = h.shape[0]
    assert conv_w.shape == (CONV_K, cwid)
    return _stream_matmul(
        "in_proj_conv", [h], [w_in], parts=(((0, cwid), (0, 2 * cwid), (0, 0)),), tc=tc, n_j=cwid // tc, tm=tm,
        epilogue=_conv_epilogue, extra=(conv_w,),
        extra_specs=(pl.BlockSpec((CONV_K, tc), lambda j, i: (0, j)),),
        out_shapes=(jax.ShapeDtypeStruct((s, cwid), BF16),),
        out_specs=(pl.BlockSpec((tm, tc), lambda j, i: (i, j)),),
        epi_scratch=(pltpu.VMEM((F32_SUBLANE, tc), F32),), n_split=2)[0]


def _pool_epilogue(accs, extra, outs, scr, pos):
    window = POOL_WINDOWS[pos.variant]
    p = accs[0]
    hist = scr[0].shape[0]
    ext = _with_history(scr[0], p, pos)
    k = 1
    while k < window:
        ext = ext + _shift_rows(ext, k)
        k *= 2
    row = jax.lax.broadcasted_iota(jnp.int32, (p.shape[0], 1), 0) + pos.row0
    count = jnp.minimum(row + 1, window).astype(F32)
    outs[0][...] = (ext[hist:] * (1.0 / count) - p).astype(outs[0].dtype)


def _in_proj_pool(h, w_in, col0, width, tm=1024):
    s = h.shape[0]
    ng = len(POOL_WINDOWS)
    assert all(w & (w - 1) == 0 and w <= POOL_HALO for w in POOL_WINDOWS)
    tc = width // ng
    return _stream_matmul(
        "in_proj_pool", [h], [w_in], parts=(((0, col0),),), tc=tc, n_j=ng, tm=tm,
        epilogue=_pool_epilogue,
        out_shapes=(jax.ShapeDtypeStruct((s, width), BF16),),
        out_specs=(pl.BlockSpec((tm, tc), lambda j, i: (i, j)),),
        epi_scratch=(pltpu.VMEM((POOL_HALO, tc), F32),), n_split=2, per_tile_variants=True)[0]


def _sigmoid(x):
    return 0.5 * jnp.tanh(0.5 * x) + 0.5


def _gate_epilogue(accs, extra, outs, scr, pos):
    outs[0][...] = _sigmoid(accs[0] + extra[0][...]).astype(outs[0].dtype)


def _ada_side_task(extra, outs, scr):
    _, c_ref, w_ref, b_ref = extra
    cs_ref, = scr

    @pl.when((pl.program_id(0) == 0) & (pl.program_id(1) == 0))
    def _():
        c = c_ref[...]
        cs = c * jax.nn.sigmoid(c)
        for r in range(0, cs_ref.shape[0], LANE):
            cs_ref[r:r + LANE, :] = jnp.broadcast_to(cs[:, r:r + LANE], (LANE, LANE)).T

    for k in range(0, w_ref.shape[1], LANE):
        sl = slice(k, k + LANE)
        outs[1][:, sl] = jnp.sum(w_ref[:, sl] * cs_ref[...], axis=0, keepdims=True) + b_ref[:, sl]


def _in_proj_gate(h, w_in, col0, bias, c_row, w_ada, b_ada, ada_col0, tm=1024, tn=1024):
    s, d = h.shape
    ncols = bias.shape[1]
    n_j, n_i = ncols // tn, s // tm
    n_ada = w_ada.shape[1] - ada_col0
    tb = n_ada // (n_j * n_i)
    assert tb * n_j * n_i == n_ada and tb % LANE == 0 and ada_col0 % tb == 0
    ada_blk = lambda j, i: (0, ada_col0 // tb + j * n_i + i)
    return _stream_matmul(
        "in_proj_gate", [h], [w_in], parts=(((0, col0),),), tc=tn, n_j=n_j, tm=tm,
        epilogue=_gate_epilogue, side_task=_ada_side_task, extra=(bias, c_row, w_ada, b_ada),
        extra_specs=(pl.BlockSpec((1, tn), lambda j, i: (0, j)),
                     pl.BlockSpec((1, d), lambda j, i: (0, 0)),
                     pl.BlockSpec((d, tb), ada_blk),
                     pl.BlockSpec((1, tb), ada_blk)),
        out_shapes=(jax.ShapeDtypeStruct((s, ncols), BF16), jax.ShapeDtypeStruct((1, n_ada), F32)),
        out_specs=(pl.BlockSpec((tm, tn), lambda j, i: (i, j)),
                   pl.BlockSpec((1, tb), lambda j, i: (0, j * n_i + i))),
        epi_scratch=(pltpu.VMEM((d, LANE), F32),))


def _fold_kernel(wg_ref, ps_ref, wpo_ref, o_ref):
    lhs = (wg_ref[0] * ps_ref[...]).astype(BF16)
    o_ref[...] = jnp.dot(lhs, wpo_ref[...].astype(BF16), preferred_element_type=F32)


def _fold_pool_weights(w_pool_group, pool_scale, w_pool_out, tn=2048):
    ng, pg, _ = w_pool_group.shape
    d = w_pool_out.shape[1]
    return pl.pallas_call(
        _fold_kernel,
        grid=(ng, d // tn),
        in_specs=[pl.BlockSpec((1, pg, pg), lambda g, n: (g, 0, 0)),
                  pl.BlockSpec((1, pg), lambda g, n: (0, g)),
                  pl.BlockSpec((pg, tn), lambda g, n: (g, n))],
        out_specs=pl.BlockSpec((pg, tn), lambda g, n: (g, n)),
        out_shape=jax.ShapeDtypeStruct((ng * pg, d), F32),
        compiler_params=_params("arbitrary", "arbitrary"),
        name="fold_pool_weights",
    )(w_pool_group, pool_scale, w_pool_out)


def _merge_epilogue(accs, extra, outs, scr, pos):
    merged = extra[0][...].astype(F32) * accs[0] + extra[1][...].astype(F32) * accs[1]
    outs[0][...] = merged.astype(outs[0].dtype)


def _merge(a, pooled, sg, w_conv_out, w_pool_fold, tm=1024, tn=1024):
    s = a.shape[0]
    d = w_conv_out.shape[1]
    nj = d // tn
    return _stream_matmul(
        "branch_merge", [a, pooled], [w_conv_out, w_pool_fold], parts=(((0, 0),), ((0, 0),)), tc=tn, n_j=nj, tm=tm,
        epilogue=_merge_epilogue, extra=(sg, sg),
        extra_specs=(pl.BlockSpec((tm, tn), lambda j, i: (i, j)),
                     pl.BlockSpec((tm, tn), lambda j, i: (i, j + nj))),
        out_shapes=(jax.ShapeDtypeStruct((s, d), BF16),),
        out_specs=(pl.BlockSpec((tm, tn), lambda j, i: (i, j)),))[0]


def _resid_epilogue(accs, extra, outs, scr, pos):
    y = accs[0]
    for acc in accs[1:]:
        y = y + acc
    outs[0][...] = extra[0][...] + extra[1][...] * y


def _matmul_resid(name, lhs, w, x, mod, gate_blk, tm, tn):
    s = lhs[0].shape[0]
    d = w.shape[1]
    goff = gate_blk * (d // tn)
    row0, parts = 0, []
    for l in lhs:
        parts.append(((row0, 0),))
        row0 += l.shape[1]
    assert row0 == w.shape[0]
    return _stream_matmul(
        name, lhs, [w] * len(lhs), parts=tuple(parts), tc=tn, n_j=d // tn, tm=tm,
        epilogue=_resid_epilogue, extra=(x, mod),
        extra_specs=(pl.BlockSpec((tm, tn), lambda j, i: (i, j)),
                     pl.BlockSpec((1, tn), lambda j, i: (0, j + goff))),
        out_shapes=(jax.ShapeDtypeStruct((s, d), F32),),
        out_specs=(pl.BlockSpec((tm, tn), lambda j, i: (i, j)),))[0]


def _swiglu_epilogue(accs, extra, outs, scr, pos):
    tc = outs[0].shape[1]
    g = accs[0][:, :tc]
    outs[0][...] = (g * _sigmoid(g) * accs[0][:, tc:]).astype(outs[0].dtype)


def _swiglu_up(h, w_gate_up, tm=1024, tc=512):
    s = h.shape[0]
    f = w_gate_up.shape[1] // 2
    bands, col0 = [], 0
    for width, n_j in ((tc, f // tc), (f % tc, 1)):
        if width and n_j:
            bands.append(_stream_matmul(
                "swiglu_up", [h], [w_gate_up], parts=(((0, col0), (0, f + col0)),), tc=width, n_j=n_j, tm=tm,
                epilogue=_swiglu_epilogue,
                out_shapes=(jax.ShapeDtypeStruct((s, width * n_j), BF16),),
                out_specs=(pl.BlockSpec((tm, width), lambda j, i: (i, j)),))[0])
            col0 += width * n_j
    return bands


def kernel(x, c, w_ada, b_ada, norm1_gain, w_in, b_branch_gate, conv_w, w_conv_out, w_pool_group,
           pool_scale, w_pool_out, w_o, norm2_gain, w_gate_up, w_down, final_norm_gain):
    batch, seq, d = x.shape
    depth = w_ada.shape[0]
    cwid = conv_w.shape[2]
    pwid = pool_scale.shape[1]
    outs = []
    for b in range(batch):
        xb = x[b]
        c_col = c[b].reshape(d, 1)
        for l in range(depth):
            b_ada_row = b_ada[l].reshape(1, -1)
            mod1 = _ada(c_col, w_ada[l], b_ada_row, 2 * d)

            h = _norm_mod(xb, norm1_gain[l].reshape(1, d), mod1, 0, 1)
            a = _in_proj_conv(h, w_in[l], conv_w[l], cwid)
            pooled = _in_proj_pool(h, w_in[l], 3 * cwid, pwid)
            sg, mod2 = _in_proj_gate(h, w_in[l], 3 * cwid + pwid, b_branch_gate[l].reshape(1, -1),
                                     c[b].reshape(1, d), w_ada[l], b_ada_row, 2 * d)
            w_pool_fold = _fold_pool_weights(w_pool_group[l], pool_scale[l].reshape(1, -1), w_pool_out[l])
            merged = _merge(a, pooled, sg, w_conv_out[l], w_pool_fold)
            xb = _matmul_resid("wo_residual", [merged], w_o[l], xb, mod2, 0, tm=512, tn=1024)

            h = _norm_mod(xb, norm2_gain[l].reshape(1, d), mod2, 1, 2)
            act_bands = _swiglu_up(h, w_gate_up[l])
            xb = _matmul_resid("down_residual", act_bands, w_down[l], xb, mod2, 3, tm=512, tn=512)
        outs.append(_norm(xb, final_norm_gain.reshape(1, d)).reshape(1, seq, d))
    return outs[0] if batch == 1 else jnp.concatenate(outs, axis=0)
```

```python
import functools
from typing import NamedTuple

import jax
import jax.numpy as jnp
from jax.experimental import pallas as pl
from jax.experimental.pallas import tpu as pltpu

F32 = jnp.float32
BF16 = jnp.bfloat16

EPS = 1e-6
CONV_K = 3
POOL_WINDOWS = (2, 4, 8, 16)
F32_SUBLANE = 8
BF16_SUBLANE = 16
LANE = 128
ADA_ROW_CHAINS = 8
POOL_HALO = 16
V7X_VMEM_LIMIT = 60 * 1024 * 1024


def _params(*sem):
    return pltpu.CompilerParams(dimension_semantics=sem, vmem_limit_bytes=V7X_VMEM_LIMIT)


def _ada_kernel(c_ref, w_ref, b_ref, o_ref):
    c = c_ref[...]
    cs = c * jax.nn.sigmoid(c)
    o_ref[...] = jnp.sum(w_ref[...] * cs, axis=0, keepdims=True) + b_ref[...]


def _ada(c_col, w_ada, b_ada, n, tn=512):
    d = w_ada.shape[0]
    return pl.pallas_call(
        _ada_kernel,
        grid=(n // tn,),
        in_specs=[pl.BlockSpec((d, 1), lambda j: (0, 0)),
                  pl.BlockSpec((d, tn), lambda j: (0, j)),
                  pl.BlockSpec((1, tn), lambda j: (0, j))],
        out_specs=pl.BlockSpec((1, tn), lambda j: (0, j)),
        out_shape=jax.ShapeDtypeStruct((1, n), F32),
        compiler_params=_params("arbitrary"),
        name="ada_matvec",
    )(c_col, w_ada, b_ada)


def _norm_mod_kernel(x_ref, g_ref, shift_ref, scale_ref, o_ref):
    x = x_ref[...]
    r = jax.lax.rsqrt(jnp.mean(x * x, axis=-1, keepdims=True) + EPS)
    y = (x * r) * g_ref[...]
    o_ref[...] = (y * (1.0 + scale_ref[...]) + shift_ref[...]).astype(o_ref.dtype)


def _norm_mod(x, gain, mod, shift_blk, scale_blk, tm=512):
    s, d = x.shape
    return pl.pallas_call(
        _norm_mod_kernel,
        grid=(s // tm,),
        in_specs=[pl.BlockSpec((tm, d), lambda i: (i, 0)),
                  pl.BlockSpec((1, d), lambda i: (0, 0)),
                  pl.BlockSpec((1, d), lambda i: (0, shift_blk)),
                  pl.BlockSpec((1, d), lambda i: (0, scale_blk))],
        out_specs=pl.BlockSpec((tm, d), lambda i: (i, 0)),
        out_shape=jax.ShapeDtypeStruct((s, d), BF16),
        compiler_params=_params("arbitrary"),
        name="norm_modulate",
    )(x, gain, mod, mod)


def _norm_kernel(x_ref, g_ref, o_ref):
    x = x_ref[...]
    r = jax.lax.rsqrt(jnp.mean(x * x, axis=-1, keepdims=True) + EPS)
    o_ref[...] = (x * r) * g_ref[...]


def _norm(x, gain, tm=512):
    s, d = x.shape
    return pl.pallas_call(
        _norm_kernel,
        grid=(s // tm,),
        in_specs=[pl.BlockSpec((tm, d), lambda i: (i, 0)),
                  pl.BlockSpec((1, d), lambda i: (0, 0))],
        out_specs=pl.BlockSpec((tm, d), lambda i: (i, 0)),
        out_shape=jax.ShapeDtypeStruct((s, d), F32),
        compiler_params=_params("arbitrary"),
        name="final_norm",
    )(x, gain)


class _TilePos(NamedTuple):
    variant: int
    row0: object
    first: object


def _chunk_copies(w_hbm, stage, sems, parts, tc, ck, tile, chunk, slot):
    copies = []
    for p, (row0, col0) in enumerate(parts):
        col = col0 + tile * tc
        if not isinstance(col, int):
            col = pl.multiple_of(col, LANE)
        row = row0 + chunk * ck
        if not isinstance(row, int):
            row = pl.multiple_of(row, BF16_SUBLANE)
        copies.append(pltpu.make_async_copy(
            w_hbm.at[pl.ds(row, ck), pl.ds(col, tc)],
            stage.at[slot, :, pl.ds(p * tc, tc)],
            sems.at[slot, p]))
    return copies


def _stream_kernel(*refs, n_ops, n_extra, n_out, parts, tc, ck, n_i, n_j, n_split, n_variants, epilogue,
                   side_task):
    lhs = refs[:n_ops]
    extra = refs[n_ops:n_ops + n_extra]
    w_hbm = refs[n_ops + n_extra:2 * n_ops + n_extra]
    outs = refs[2 * n_ops + n_extra:2 * n_ops + n_extra + n_out]
    scratch = refs[2 * n_ops + n_extra + n_out:]
    wbf, stage, sems = (scratch[k:3 * n_ops:3] for k in range(3))
    epi_scratch = scratch[3 * n_ops:]
    j = pl.program_id(0)
    i = pl.program_id(1)
    cur = j % 2

    def copies(o, tile, chunk, slot):
        return _chunk_copies(w_hbm[o], stage[o], sems[o], parts[o], tc, ck[o], tile, chunk, slot)

    @pl.when((j == 0) & (i == 0))
    def _load_first_tile():
        for ref in epi_scratch:
            ref[...] = jnp.zeros_like(ref)
        for o in range(n_ops):
            for cp in copies(o, 0, 0, 0):
                cp.start()
            for c in range(n_i):
                if c + 1 < n_i:
                    for cp in copies(o, 0, c + 1, (c + 1) % 2):
                        cp.start()
                for cp in copies(o, 0, c, c % 2):
                    cp.wait()
                wbf[o][0, c * ck[o]:(c + 1) * ck[o], :] = stage[o][c % 2].astype(BF16)
            if n_j > 1:
                for cp in copies(o, 1, 0, 0):
                    cp.start()

    if side_task is not None:
        side_task(extra, outs, epi_scratch)

    @pl.when(j + 1 < n_j)
    def _stream_next_tile():
        slot = i % 2
        for o in range(n_ops):
            for cp in copies(o, j + 1, i, slot):
                cp.wait()

            @pl.when(i + 1 < n_i)
            def _():
                for cp in copies(o, j + 1, i + 1, 1 - slot):
                    cp.start()

            @pl.when((i + 1 == n_i) & (j + 2 < n_j))
            def _():
                for cp in copies(o, j + 2, 0, 0):
                    cp.start()

            row = pl.multiple_of(i * ck[o], BF16_SUBLANE)
            wbf[o][1 - cur, pl.ds(row, ck[o]), :] = stage[o][slot].astype(BF16)

    tm = lhs[0].shape[0]
    rows = tm // n_split

    def compute(variant):
        for m in range(n_split):
            sl = pl.ds(m * rows, rows)
            accs = [jnp.dot(lhs[o][sl, :], wbf[o][cur], preferred_element_type=F32) for o in range(n_ops)]
            pos = _TilePos(variant=variant, row0=i * tm + m * rows, first=(i == 0) if m == 0 else False)
            epilogue(accs, [e.at[sl, :] if e.shape[0] == tm else e for e in extra],
                     [out.at[sl, :] if out.shape[0] == tm else out for out in outs], epi_scratch, pos)

    if n_variants == 1:
        compute(0)
    else:
        for v in range(n_variants):
            pl.when(j == v)(functools.partial(compute, v))


def _stream_matmul(name, lhs, weights, parts, tc, n_j, tm, epilogue, extra=(), extra_specs=(),
                   out_shapes=(), out_specs=(), epi_scratch=(), n_split=1, per_tile_variants=False,
                   side_task=None):
    n_ops = len(lhs)
    s = lhs[0].shape[0]
    n_i = s // tm
    assert s % tm == 0 and n_i % 2 == 0, "stage slots alternate per row tile"
    ck = []
    in_specs = [pl.BlockSpec((tm, l.shape[1]), lambda j, i: (i, 0)) for l in lhs]
    in_specs += list(extra_specs)
    in_specs += [pl.BlockSpec(memory_space=pl.ANY) for _ in weights]
    scratch = []
    for o in range(n_ops):
        k = lhs[o].shape[1]
        assert lhs[o].shape[0] == s and k % (n_i * BF16_SUBLANE) == 0
        ck.append(k // n_i)
        tn = len(parts[o]) * tc
        scratch += [pltpu.VMEM((2, k, tn), BF16),
                    pltpu.VMEM((2, ck[o], tn), F32),
                    pltpu.SemaphoreType.DMA((2, len(parts[o])))]
    body = functools.partial(_stream_kernel, n_ops=n_ops, n_extra=len(extra), n_out=len(out_shapes),
                             parts=parts, tc=tc, ck=tuple(ck), n_i=n_i, n_j=n_j, n_split=n_split,
                             n_variants=n_j if per_tile_variants else 1, epilogue=epilogue, side_task=side_task)
    return pl.pallas_call(
        body,
        grid=(n_j, n_i),
        in_specs=in_specs,
        out_specs=list(out_specs),
        out_shape=list(out_shapes),
        scratch_shapes=scratch + list(epi_scratch),
        compiler_params=_params("arbitrary", "arbitrary"),
        name=name,
    )(*lhs, *extra, *weights)


def _shift_rows(ext, k):
    return pltpu.roll(ext, k, axis=0)


def _with_history(carry_ref, cur, pos):
    hist = carry_ref[...]
    if pos.first is not False:
        hist = jnp.where(pos.first, 0.0, hist)
    carry_ref[...] = cur[-carry_ref.shape[0]:]
    return jnp.concatenate([hist, cur], axis=0)


def _conv_epilogue(accs, extra, outs, scr, pos):
    tc = outs[0].shape[1]
    cw_ref, = extra
    acc = accs[0]
    u = acc[:, :tc] * acc[:, tc:2 * tc]
    hist = scr[0].shape[0]
    ext = _with_history(scr[0], u, pos)
    y = cw_ref[0:1, :] * _shift_rows(ext, 2)[hist:]
    y = y + cw_ref[1:2, :] * _shift_rows(ext, 1)[hist:]
    y = y + cw_ref[2:3, :] * u
    outs[0][...] = (acc[:, 2 * tc:] * y).astype(outs[0].dtype)


def _in_proj_conv(h, w_in, conv_w, cwid, tm=1024, tc=512):
    s = h.shape[0]
    assert conv_w.shape == (CONV_K, cwid)
    return _stream_matmul(
        "in_proj_conv", [h], [w_in], parts=(((0, cwid), (0, 2 * cwid), (0, 0)),), tc=tc, n_j=cwid // tc, tm=tm,
        epilogue=_conv_epilogue, extra=(conv_w,),
        extra_specs=(pl.BlockSpec((CONV_K, tc), lambda j, i: (0, j)),),
        out_shapes=(jax.ShapeDtypeStruct((s, cwid), BF16),),
        out_specs=(pl.BlockSpec((tm, tc), lambda j, i: (i, j)),),
        epi_scratch=(pltpu.VMEM((F32_SUBLANE, tc), F32),), n_split=2)[0]


def _pool_epilogue(accs, extra, outs, scr, pos):
    window = POOL_WINDOWS[pos.variant]
    p = accs[0]
    hist = scr[0].shape[0]
    ext = _with_history(scr[0], p, pos)
    k = 1
    while k < window:
        ext = ext + _shift_rows(ext, k)
        k *= 2
    row = jax.lax.broadcasted_iota(jnp.int32, (p.shape[0], 1), 0) + pos.row0
    count = jnp.minimum(row + 1, window).astype(F32)
    outs[0][...] = (ext[hist:] * (1.0 / count) - p).astype(outs[0].dtype)


def _in_proj_pool(h, w_in, col0, width, tm=2048):
    s = h.shape[0]
    ng = len(POOL_WINDOWS)
    assert all(w & (w - 1) == 0 and w <= POOL_HALO for w in POOL_WINDOWS)
    tc = width // ng
    return _stream_matmul(
        "in_proj_pool", [h], [w_in], parts=(((0, col0),),), tc=tc, n_j=ng, tm=tm,
        epilogue=_pool_epilogue,
        out_shapes=(jax.ShapeDtypeStruct((s, width), BF16),),
        out_specs=(pl.BlockSpec((tm, tc), lambda j, i: (i, j)),),
        epi_scratch=(pltpu.VMEM((POOL_HALO, tc), F32),), n_split=4, per_tile_variants=True)[0]


def _sigmoid(x):
    return 0.5 * jnp.tanh(0.5 * x) + 0.5


def _gate_epilogue(accs, extra, outs, scr, pos):
    outs[0][...] = _sigmoid(accs[0] + extra[0][...]).astype(outs[0].dtype)


def _ada_side_task(extra, outs, scr):
    _, c_ref, w_ref, b_ref = extra
    cs_ref, = scr

    @pl.when((pl.program_id(0) == 0) & (pl.program_id(1) == 0))
    def _():
        c = c_ref[...]
        cs = c * jax.nn.sigmoid(c)
        for r in range(0, cs_ref.shape[0], LANE):
            cs_ref[r:r + LANE, :] = jnp.broadcast_to(cs[:, r:r + LANE], (LANE, LANE)).T

    d = w_ref.shape[0]
    rows = d // ADA_ROW_CHAINS
    for k in range(0, w_ref.shape[1], LANE):
        sl = slice(k, k + LANE)
        parts = [jnp.sum(w_ref[r:r + rows, sl] * cs_ref[r:r + rows, :], axis=0, keepdims=True)
                 for r in range(0, d, rows)]
        while len(parts) > 1:
            parts = [parts[q] + parts[q + 1] for q in range(0, len(parts), 2)]
        outs[1][:, sl] = parts[0] + b_ref[:, sl]


def _in_proj_gate(h, w_in, col0, bias, c_row, w_ada, b_ada, ada_col0, tm=1024, tn=1024):
    s, d = h.shape
    ncols = bias.shape[1]
    n_j, n_i = ncols // tn, s // tm
    n_ada = w_ada.shape[1] - ada_col0
    tb = n_ada // (n_j * n_i)
    assert tb * n_j * n_i == n_ada and tb % LANE == 0 and ada_col0 % tb == 0
    ada_blk = lambda j, i: (0, ada_col0 // tb + j * n_i + i)
    return _stream_matmul(
        "in_proj_gate", [h], [w_in], parts=(((0, col0),),), tc=tn, n_j=n_j, tm=tm,
        epilogue=_gate_epilogue, side_task=_ada_side_task, extra=(bias, c_row, w_ada, b_ada),
        extra_specs=(pl.BlockSpec((1, tn), lambda j, i: (0, j)),
                     pl.BlockSpec((1, d), lambda j, i: (0, 0)),
                     pl.BlockSpec((d, tb), ada_blk),
                     pl.BlockSpec((1, tb), ada_blk)),
        out_shapes=(jax.ShapeDtypeStruct((s, ncols), BF16), jax.ShapeDtypeStruct((1, n_ada), F32)),
        out_specs=(pl.BlockSpec((tm, tn), lambda j, i: (i, j)),
                   pl.BlockSpec((1, tb), lambda j, i: (0, j * n_i + i))),
        epi_scratch=(pltpu.VMEM((d, LANE), F32),))


def _fold_kernel(wg_ref, ps_ref, wpo_ref, o_ref):
    lhs = (wg_ref[0] * ps_ref[...]).astype(BF16)
    o_ref[...] = jnp.dot(lhs, wpo_ref[...].astype(BF16), preferred_element_type=F32)


def _fold_pool_weights(w_pool_group, pool_scale, w_pool_out, tn=2048):
    ng, pg, _ = w_pool_group.shape
    d = w_pool_out.shape[1]
    return pl.pallas_call(
        _fold_kernel,
        grid=(ng, d // tn),
        in_specs=[pl.BlockSpec((1, pg, pg), lambda g, n: (g, 0, 0)),
                  pl.BlockSpec((1, pg), lambda g, n: (0, g)),
                  pl.BlockSpec((pg, tn), lambda g, n: (g, n))],
        out_specs=pl.BlockSpec((pg, tn), lambda g, n: (g, n)),
        out_shape=jax.ShapeDtypeStruct((ng * pg, d), F32),
        compiler_params=_params("arbitrary", "arbitrary"),
        name="fold_pool_weights",
    )(w_pool_group, pool_scale, w_pool_out)


def _merge_epilogue(accs, extra, outs, scr, pos):
    merged = extra[0][...].astype(F32) * accs[0] + extra[1][...].astype(F32) * accs[1]
    outs[0][...] = merged.astype(outs[0].dtype)


def _merge(a, pooled, sg, w_conv_out, w_pool_fold, tm=1024, tn=1024):
    s = a.shape[0]
    d = w_conv_out.shape[1]
    nj = d // tn
    return _stream_matmul(
        "branch_merge", [a, pooled], [w_conv_out, w_pool_fold], parts=(((0, 0),), ((0, 0),)), tc=tn, n_j=nj, tm=tm,
        epilogue=_merge_epilogue, extra=(sg, sg),
        extra_specs=(pl.BlockSpec((tm, tn), lambda j, i: (i, j)),
                     pl.BlockSpec((tm, tn), lambda j, i: (i, j + nj))),
        out_shapes=(jax.ShapeDtypeStruct((s, d), BF16),),
        out_specs=(pl.BlockSpec((tm, tn), lambda j, i: (i, j)),))[0]


def _resid_epilogue(accs, extra, outs, scr, pos):
    y = accs[0]
    for acc in accs[1:]:
        y = y + acc
    outs[0][...] = extra[0][...] + extra[1][...] * y


def _matmul_resid(name, lhs, w, x, mod, gate_blk, tm, tn):
    s = lhs[0].shape[0]
    d = w.shape[1]
    goff = gate_blk * (d // tn)
    row0, parts = 0, []
    for l in lhs:
        parts.append(((row0, 0),))
        row0 += l.shape[1]
    assert row0 == w.shape[0]
    return _stream_matmul(
        name, lhs, [w] * len(lhs), parts=tuple(parts), tc=tn, n_j=d // tn, tm=tm,
        epilogue=_resid_epilogue, extra=(x, mod),
        extra_specs=(pl.BlockSpec((tm, tn), lambda j, i: (i, j)),
                     pl.BlockSpec((1, tn), lambda j, i: (0, j + goff))),
        out_shapes=(jax.ShapeDtypeStruct((s, d), F32),),
        out_specs=(pl.BlockSpec((tm, tn), lambda j, i: (i, j)),))[0]


def _swiglu_epilogue(accs, extra, outs, scr, pos):
    tc = outs[0].shape[1]
    g = accs[0][:, :tc]
    outs[0][...] = (g * _sigmoid(g) * accs[0][:, tc:]).astype(outs[0].dtype)


def _swiglu_up(h, w_gate_up, tm=1024, tc=512):
    s = h.shape[0]
    f = w_gate_up.shape[1] // 2
    bands, col0 = [], 0
    for width, n_j in ((tc, f // tc), (f % tc, 1)):
        if width and n_j:
            bands.append(_stream_matmul(
                "swiglu_up", [h], [w_gate_up], parts=(((0, col0), (0, f + col0)),), tc=width, n_j=n_j, tm=tm,
                epilogue=_swiglu_epilogue,
                out_shapes=(jax.ShapeDtypeStruct((s, width * n_j), BF16),),
                out_specs=(pl.BlockSpec((tm, width), lambda j, i: (i, j)),))[0])
            col0 += width * n_j
    return bands


def kernel(x, c, w_ada, b_ada, norm1_gain, w_in, b_branch_gate, conv_w, w_conv_out, w_pool_group,
           pool_scale, w_pool_out, w_o, norm2_gain, w_gate_up, w_down, final_norm_gain):
    batch, seq, d = x.shape
    depth = w_ada.shape[0]
    cwid = conv_w.shape[2]
    pwid = pool_scale.shape[1]
    outs = []
    for b in range(batch):
        xb = x[b]
        c_col = c[b].reshape(d, 1)
        for l in range(depth):
            b_ada_row = b_ada[l].reshape(1, -1)
            mod1 = _ada(c_col, w_ada[l], b_ada_row, 2 * d)

            h = _norm_mod(xb, norm1_gain[l].reshape(1, d), mod1, 0, 1)
            a = _in_proj_conv(h, w_in[l], conv_w[l], cwid)
            pooled = _in_proj_pool(h, w_in[l], 3 * cwid, pwid)
            sg, mod2 = _in_proj_gate(h, w_in[l], 3 * cwid + pwid, b_branch_gate[l].reshape(1, -1),
                                     c[b].reshape(1, d), w_ada[l], b_ada_row, 2 * d)
            w_pool_fold = _fold_pool_weights(w_pool_group[l], pool_scale[l].reshape(1, -1), w_pool_out[l])
            merged = _merge(a, pooled, sg, w_conv_out[l], w_pool_fold)
            xb = _matmul_resid("wo_residual", [merged], w_o[l], xb, mod2, 0, tm=1024, tn=1024)

            h = _norm_mod(xb, norm2_gain[l].reshape(1, d), mod2, 1, 2)
            act_bands = _swiglu_up(h, w_gate_up[l])
            xb = _matmul_resid("down_residual", act_bands, w_down[l], xb, mod2, 3, tm=512, tn=512)
        outs.append(_norm(xb, final_norm_gain.reshape(1, d)).reshape(1, seq, d))
    return outs[0] if batch == 1 else jnp.concatenate(outs, axis=0)
```
